```python
import math
import jax, jax.numpy as jnp
from jax import lax
import numpy as np

D_MODEL = 1024
BATCH = 2
SEQ = 8192
DEPTH = 2
DEC_BATCH = 128
DEC_SEQ = 1
PAST_LEN = 2048
PAGE_SIZE = 128

PLE_DIM = 256
D_FF = 2816
A_HEADS = 4
A_HEAD_DIM = D_MODEL // 16
A_WIDTH = A_HEADS * 2 * A_HEAD_DIM
B_WIDTH = D_MODEL // 4
B_HEADS = 4
B_HEAD_DIM = B_WIDTH // B_HEADS
CHUNK = 128
C_WIDTH = D_MODEL // 4
POOL_WINDOWS = (2, 4, 8, 16)
C_GROUPS = len(POOL_WINDOWS)
C_GROUP_DIM = C_WIDTH // C_GROUPS
POOL_BUF = max(POOL_WINDOWS) - 1
D_MIX = A_WIDTH + B_WIDTH + C_WIDTH
D_IN = 3 * A_WIDTH + 2 * B_WIDTH + C_WIDTH
Q_BLOCK = 128
ROPE_THETA = 10000.0
EPS = 1e-6
ATTN_SCALE = A_HEAD_DIM ** -0.5

kernel_name = 'hybrid_diffattn_sgu_pool_decoder_step'


def _rmsnorm(x, g):
    xf = x.astype(jnp.float32)
    y = xf * lax.rsqrt(jnp.mean(xf * xf, axis=-1, keepdims=True) + EPS) * g.astype(jnp.float32)
    return y.astype(x.dtype)


def _swiglu(x, g, w1, w3, w2):
    h = _rmsnorm(x, g)
    return (jax.nn.silu(h @ w1) * (h @ w3)) @ w2


def _rope(x, pos):
    half = A_HEAD_DIM // 2
    inv = ROPE_THETA ** (-jnp.arange(half, dtype=jnp.float32) / half)
    ang = pos.astype(jnp.float32)[:, None] * inv[None, :]
    cos = jnp.cos(ang)[None, :, None, None, :]
    sin = jnp.sin(ang)[None, :, None, None, :]
    xf = x.astype(jnp.float32)
    x1, x2 = xf[..., :half], xf[..., half:]
    return jnp.concatenate([x1 * cos - x2 * sin, x2 * cos + x1 * sin], axis=-1).astype(x.dtype)


def _split_in(h):
    bn, t = h.shape[0], h.shape[1]
    o = 0
    q = h[..., o:o + A_WIDTH].reshape(bn, t, A_HEADS, 2, A_HEAD_DIM); o += A_WIDTH
    k = h[..., o:o + A_WIDTH].reshape(bn, t, A_HEADS, 2, A_HEAD_DIM); o += A_WIDTH
    v = h[..., o:o + A_WIDTH].reshape(bn, t, A_HEADS, 2 * A_HEAD_DIM); o += A_WIDTH
    hu = h[..., o:o + B_WIDTH]; o += B_WIDTH
    hv = h[..., o:o + B_WIDTH]; o += B_WIDTH
    c = h[..., o:o + C_WIDTH]
    return q, k, v, hu, hv, c


def _lambda(lam_qk_l, lam_init):
    lq = lam_qk_l.astype(jnp.float32)
    return jnp.exp(jnp.sum(lq[0] * lq[1])) - jnp.exp(jnp.sum(lq[2] * lq[3])) + lam_init


def _diff_attend(q, k, v, q_pos, k_pos, lam):
    s = jnp.einsum('bqhcd,bkhcd->bhcqk', q, k).astype(jnp.float32) * ATTN_SCALE
    causal = k_pos[None, :] <= q_pos[:, None]
    p = jax.nn.softmax(jnp.where(causal, s, -jnp.inf), axis=-1)
    a = p[:, :, 0] - lam * p[:, :, 1]
    return jnp.einsum('bhqk,bkhe->bqhe', a.astype(v.dtype), v)


def _diff_attn_prompt(q, k, v, pos, lam):
    bn, t = q.shape[0], q.shape[1]
    nb = t // Q_BLOCK
    qb = jnp.moveaxis(q.reshape(bn, nb, Q_BLOCK, A_HEADS, 2, A_HEAD_DIM), 1, 0)
    pb = pos.reshape(nb, Q_BLOCK)
    o = lax.map(lambda xs: _diff_attend(xs[0], k, v, xs[1], pos, lam), (qb, pb))
    return jnp.moveaxis(o, 0, 1).reshape(bn, t, A_HEADS, 2 * A_HEAD_DIM)


def _subln(o, g, lam_init):
    bn, t = o.shape[0], o.shape[1]
    return (_rmsnorm(o, g) * (1.0 - lam_init)).reshape(bn, t, A_WIDTH)


def _sgu_prompt(hu, hv, g, w_s, b_s):
    bn, t = hu.shape[0], hu.shape[1]
    u = jax.nn.gelu(hu)
    v = _rmsnorm(jax.nn.gelu(hv), g)
    vc = v.reshape(bn, t // CHUNK, CHUNK, B_HEADS, B_HEAD_DIM)
    wm = w_s * jnp.tril(jnp.ones((CHUNK, CHUNK), w_s.dtype))
    z = jnp.einsum('hts,bcshd->bcthd', wm, vc) + b_s.T[None, None, :, :, None]
    return u * z.reshape(bn, t, B_WIDTH)


def _sgu_sample(hu, hv, g, w_s, b_s):
    bn, t = hu.shape[0], hu.shape[1]
    u = jax.nn.gelu(hu)
    v = _rmsnorm(jax.nn.gelu(hv), g)
    wm = (w_s * jnp.tril(jnp.ones((CHUNK, CHUNK), w_s.dtype)))[:, :t, :t]
    z = jnp.einsum('hts,bshd->bthd', wm, v.reshape(bn, t, B_HEADS, B_HEAD_DIM)) + b_s[:, :t].T[None, :, :, None]
    return u * z.reshape(bn, t, B_WIDTH), v


def _pool_mix(ext, w_pool, s_pool):
    bn, L = ext.shape[0], ext.shape[1]
    xf = ext.astype(jnp.float32)
    cs = jnp.concatenate([jnp.zeros_like(xf[:, :1]), jnp.cumsum(xf, axis=1)], axis=1)
    idx = jnp.arange(L)
    parts = []
    for gi, w in enumerate(POOL_WINDOWS):
        sl = slice(gi * C_GROUP_DIM, (gi + 1) * C_GROUP_DIM)
        lo = jnp.maximum(idx + 1 - w, 0)
        csg = cs[..., sl]
        cnt = (idx + 1 - lo).astype(jnp.float32)[None, :, None]
        parts.append((csg[:, 1:] - csg[:, lo]) / cnt - xf[..., sl])
    d = jnp.stack(parts, axis=2)
    y = jnp.einsum('blgc,gcd->blgd', d, w_pool.astype(jnp.float32)).reshape(bn, L, C_WIDTH)
    return (y * s_pool.astype(jnp.float32)).astype(ext.dtype)


def _ple(x, pe, g, w_pg, w_ple):
    return jax.nn.sigmoid(_rmsnorm(x, g) @ w_pg) * (pe @ w_ple)


def setup_inputs(seed: int = 0) -> dict:
    key = jax.random.key(seed)
    ks = jax.random.split(key, 40)
    f32 = jnp.float32

    def nrm(k, shape, scale):
        return jax.random.normal(k, shape, f32) * scale

    def gain(k, shape):
        return 1.0 + 0.05 * jax.random.normal(k, shape, f32)

    n_pages = PAST_LEN // PAGE_SIZE
    n_pool = (DEC_BATCH * n_pages * 5) // 4
    page_table = jax.random.permutation(ks[0], n_pool)[:DEC_BATCH * n_pages].reshape(DEC_BATCH, n_pages).astype(jnp.int32)
    L = DEPTH
    return {
        'x_prompt': nrm(ks[1], (BATCH, SEQ, D_MODEL), 1.0),
        'x_sample': nrm(ks[2], (DEC_BATCH, DEC_SEQ, D_MODEL), 1.0),
        'cache_k': nrm(ks[3], (L, n_pool, PAGE_SIZE, A_HEADS, 2, A_HEAD_DIM), 1.0),
        'cache_v': nrm(ks[4], (L, n_pool, PAGE_SIZE, A_HEADS, 2 * A_HEAD_DIM), 1.0),
        'state_pool': nrm(ks[5], (L, DEC_BATCH, POOL_BUF, C_WIDTH), 1.0),
        'page_table': page_table,
        'p_prompt': nrm(ks[6], (L, BATCH, SEQ, PLE_DIM), 1.0),
        'p_sample': nrm(ks[7], (L, DEC_BATCH, DEC_SEQ, PLE_DIM), 1.0),
        'g_ffn1': gain(ks[8], (L, D_MODEL)),
        'w1_ffn1': nrm(ks[9], (L, D_MODEL, D_FF), D_MODEL ** -0.5),
        'w3_ffn1': nrm(ks[10], (L, D_MODEL, D_FF), D_MODEL ** -0.5),
        'w2_ffn1': nrm(ks[11], (L, D_FF, D_MODEL), D_FF ** -0.5),
        'g_mix': gain(ks[12], (L, D_MODEL)),
        'w_in': nrm(ks[13], (L, D_MODEL, D_IN), D_MODEL ** -0.5),
        'lam_qk': nrm(ks[14], (L, 4, A_HEAD_DIM), 0.1),
        'g_subln': gain(ks[15], (L, 2 * A_HEAD_DIM)),
        'g_sgu': gain(ks[16], (L, B_WIDTH)),
        'w_sgu': nrm(ks[17], (L, B_HEADS, CHUNK, CHUNK), CHUNK ** -0.5),
        'b_sgu': 1.0 + nrm(ks[18], (L, B_HEADS, CHUNK), 0.01),
        'w_pool': nrm(ks[19], (L, C_GROUPS, C_GROUP_DIM, C_GROUP_DIM), C_GROUP_DIM ** -0.5),
        's_pool': gain(ks[20], (L, C_WIDTH)),
        'w_out': nrm(ks[21], (L, D_MIX, D_MODEL), D_MIX ** -0.5),
        'g_ffn2': gain(ks[22], (L, D_MODEL)),
        'w1_ffn2': nrm(ks[23], (L, D_MODEL, D_FF), D_MODEL ** -0.5),
        'w3_ffn2': nrm(ks[24], (L, D_MODEL, D_FF), D_MODEL ** -0.5),
        'w2_ffn2': nrm(ks[25], (L, D_FF, D_MODEL), D_FF ** -0.5),
        'g_ple': gain(ks[26], (L, D_MODEL)),
        'w_pg': nrm(ks[27], (L, D_MODEL, D_MODEL), D_MODEL ** -0.5),
        'w_ple': nrm(ks[28], (L, PLE_DIM, D_MODEL), PLE_DIM ** -0.5),
        'g_final': gain(ks[29], (D_MODEL,)),
    }


def reference(x_prompt, x_sample, cache_k, cache_v, state_pool, page_table, p_prompt, p_sample,
              g_ffn1, w1_ffn1, w3_ffn1, w2_ffn1, g_mix, w_in, lam_qk, g_subln, g_sgu, w_sgu, b_sgu,
              w_pool, s_pool, w_out, g_ffn2, w1_ffn2, w3_ffn2, w2_ffn2, g_ple, w_pg, w_ple, g_final):
    pos_p = jnp.arange(x_prompt.shape[1], dtype=jnp.int32)
    pos_s = PAST_LEN + jnp.arange(x_sample.shape[1], dtype=jnp.int32)
    n_past = page_table.shape[1] * PAGE_SIZE
    k_pos_s = jnp.arange(n_past + x_sample.shape[1], dtype=jnp.int32)
    xp, xs = x_prompt, x_sample
    kp_l, vp_l, ks_l, vs_l, pp_l, ps_l, sv_l = [], [], [], [], [], [], []
    for l in range(DEPTH):
        lam_init = 0.8 - 0.6 * math.exp(-0.3 * l)
        lam = _lambda(lam_qk[l], lam_init)

        xp = xp + 0.5 * _swiglu(xp, g_ffn1[l], w1_ffn1[l], w3_ffn1[l], w2_ffn1[l])
        q, k, v, hu, hv, c = _split_in(_rmsnorm(xp, g_mix[l]) @ w_in[l])
        q = _rope(q, pos_p)
        k = _rope(k, pos_p)
        a = _subln(_diff_attn_prompt(q, k, v, pos_p, lam), g_subln[l], lam_init)
        bo = _sgu_prompt(hu, hv, g_sgu[l], w_sgu[l], b_sgu[l])
        co = _pool_mix(c, w_pool[l], s_pool[l])
        xp = xp + jnp.concatenate([a, bo, co], axis=-1) @ w_out[l]
        xp = xp + 0.5 * _swiglu(xp, g_ffn2[l], w1_ffn2[l], w3_ffn2[l], w2_ffn2[l])
        xp = xp + _ple(xp, p_prompt[l], g_ple[l], w_pg[l], w_ple[l])
        kp_l.append(k)
        vp_l.append(v)
        pp_l.append(c[:, -POOL_BUF:])

        xs = xs + 0.5 * _swiglu(xs, g_ffn1[l], w1_ffn1[l], w3_ffn1[l], w2_ffn1[l])
        q, k, v, hu, hv, c = _split_in(_rmsnorm(xs, g_mix[l]) @ w_in[l])
        q = _rope(q, pos_s)
        k = _rope(k, pos_s)
        k_past = cache_k[l, page_table].reshape(DEC_BATCH, n_past, A_HEADS, 2, A_HEAD_DIM)
        v_past = cache_v[l, page_table].reshape(DEC_BATCH, n_past, A_HEADS, 2 * A_HEAD_DIM)
        kk = jnp.concatenate([k_past, k.astype(k_past.dtype)], axis=1)
        vv = jnp.concatenate([v_past, v.astype(v_past.dtype)], axis=1)
        a = _subln(_diff_attend(q, kk, vv, pos_s, k_pos_s, lam), g_subln[l], lam_init)
        bo, sv = _sgu_sample(hu, hv, g_sgu[l], w_sgu[l], b_sgu[l])
        ext = jnp.concatenate([state_pool[l].astype(c.dtype), c], axis=1)
        co = _pool_mix(ext, w_pool[l], s_pool[l])[:, -c.shape[1]:]
        xs = xs + jnp.concatenate([a.astype(xs.dtype), bo, co], axis=-1) @ w_out[l]
        xs = xs + 0.5 * _swiglu(xs, g_ffn2[l], w1_ffn2[l], w3_ffn2[l], w2_ffn2[l])
        xs = xs + _ple(xs, p_sample[l], g_ple[l], w_pg[l], w_ple[l])
        ks_l.append(k)
        vs_l.append(v)
        ps_l.append(ext[:, -POOL_BUF:])
        sv_l.append(sv)

    y_prompt = _rmsnorm(xp, g_final)
    y_sample = _rmsnorm(xs, g_final)
    k_prompt = jnp.stack(kp_l)
    v_prompt = jnp.stack(vp_l)
    k_sample = jnp.stack(ks_l)
    v_sample = jnp.stack(vs_l)
    pool_prompt = jnp.stack(pp_l)
    pool_sample = jnp.stack(ps_l)
    sgu_v_sample = jnp.stack(sv_l)
    return (y_prompt, y_sample, k_prompt, v_prompt, k_sample, v_sample, pool_prompt, pool_sample, sgu_v_sample)
```

```python
import functools
import math

import jax
import jax.numpy as jnp
from jax import lax
from jax.experimental import pallas as pl
from jax.experimental.pallas import tpu as pltpu

F32 = jnp.float32
BF16 = jnp.bfloat16

EPS = 1e-6
ROPE_THETA = 10000.0
A_HEADS = 4
A_HEAD_DIM = 64
HEAD_W = 2 * A_HEAD_DIM
A_WIDTH = A_HEADS * HEAD_W
B_WIDTH = 256
B_HEADS = 4
CHUNK = 128
C_WIDTH = 256
POOL_WINDOWS = (2, 4, 8, 16)
C_GROUP_DIM = C_WIDTH // len(POOL_WINDOWS)
HALO = 16
POOL_BUF = 15
ATTN_SCALE = A_HEAD_DIM ** -0.5
PAGE_SIZE = 128

VMEM_LIMIT_BYTES = 56 * 1024 * 1024


def _lam_init(layer):
    return 0.8 - 0.6 * math.exp(-0.3 * layer)


def _rms(x, g):
    return x * lax.rsqrt(jnp.mean(x * x, axis=-1, keepdims=True) + EPS) * g


def _mm(a, w):
    return jnp.dot(a.astype(BF16), w, preferred_element_type=F32)


def _ffn(x, g, w1, w3, w2):
    h = _rms(x, g).astype(BF16)
    a = jnp.dot(h, w1, preferred_element_type=F32)
    b = jnp.dot(h, w3, preferred_element_type=F32)
    return x + 0.5 * _mm(a * jax.nn.sigmoid(a) * b, w2)


def _rope(z, cos, sin_signed):
    m = z.shape[0]
    lane = lax.broadcasted_iota(jnp.int32, (m, HEAD_W), 1)
    first_half = (lane % A_HEAD_DIM) < (A_HEAD_DIM // 2)
    outs = []
    for j in range(z.shape[1] // HEAD_W):
        blk = z[:, j * HEAD_W:(j + 1) * HEAD_W]
        from_right = pltpu.roll(blk, HEAD_W - A_HEAD_DIM // 2, axis=1)
        from_left = pltpu.roll(blk, A_HEAD_DIM // 2, axis=1)
        partner = jnp.where(first_half, from_right, from_left)
        outs.append(blk * cos + partner * sin_signed)
    return jnp.concatenate(outs, axis=1)


def _pool_window_lanes(shape):
    lane = lax.broadcasted_iota(jnp.int32, shape, len(shape) - 1)
    w = jnp.full(shape, POOL_WINDOWS[-1], jnp.int32)
    for gi in range(len(POOL_WINDOWS) - 2, -1, -1):
        w = jnp.where(lane < (gi + 1) * C_GROUP_DIM, POOL_WINDOWS[gi], w)
    return w


def _pool_weight(wpool_tiled):
    r = lax.broadcasted_iota(jnp.int32, wpool_tiled.shape, 0) // C_GROUP_DIM
    c = lax.broadcasted_iota(jnp.int32, wpool_tiled.shape, 1) // C_GROUP_DIM
    return jnp.where(r == c, wpool_tiled, 0.0).astype(BF16)


def _split_in(z):
    o = 0
    q = z[:, o:o + A_WIDTH]; o += A_WIDTH
    k = z[:, o:o + A_WIDTH]; o += A_WIDTH
    v = z[:, o:o + A_WIDTH]; o += A_WIDTH
    hu = z[:, o:o + B_WIDTH]; o += B_WIDTH
    hv = z[:, o:o + B_WIDTH]; o += B_WIDTH
    c = z[:, o:o + C_WIDTH]
    return q, k, v, hu, hv, c


def _pre_prompt_kernel(x_ref, cos_ref, sin_ref, g1_ref, w1_ref, w3_ref, w2_ref, gm_ref, win_ref,
                       gs_ref, wcat_ref, bfull_ref, wpool_ref, spool_ref,
                       xmid_ref, qb_ref, ktb_ref, vb_ref, ktf_ref, vf_ref, bc_ref, ctail_ref,
                       cext_ref, *, tm, tiles_per_seq):
    i = pl.program_id(0)
    xm = _ffn(x_ref[...], g1_ref[...], w1_ref[...], w3_ref[...], w2_ref[...])
    xmid_ref[...] = xm
    z = _mm(_rms(xm, gm_ref[...]), win_ref[...])
    q, k, v, hu, hv, c = _split_in(z)
    cos = cos_ref[...]
    sin = sin_ref[...]
    q = _rope(q, cos, sin)
    k = _rope(k, cos, sin)
    qb_ref[...] = (q * ATTN_SCALE).astype(BF16)
    kt = k.T
    ktf_ref[...] = kt
    ktb_ref[...] = kt.astype(BF16)
    vb_ref[...] = v.astype(BF16)
    vf_ref[...] = v

    u = jax.nn.gelu(hu)
    vn = _rms(jax.nn.gelu(hv), gs_ref[...])
    wr = lax.broadcasted_iota(jnp.int32, (CHUNK, B_HEADS * CHUNK), 0)
    wc = lax.broadcasted_iota(jnp.int32, (CHUNK, B_HEADS * CHUNK), 1)
    wm = jnp.where((wc % CHUNK) <= wr, wcat_ref[...], 0.0).astype(BF16)
    hr = lax.broadcasted_iota(jnp.int32, (B_HEADS * CHUNK, B_WIDTH), 0) // CHUNK
    hc = lax.broadcasted_iota(jnp.int32, (B_HEADS * CHUNK, B_WIDTH), 1) // (B_WIDTH // B_HEADS)
    head_mask = hr == hc
    bfull = bfull_ref[...]
    for ci in range(tm // CHUNK):
        rows = slice(ci * CHUNK, (ci + 1) * CHUNK)
        vc = vn[rows].astype(BF16)
        vbd = jnp.where(head_mask, jnp.concatenate([vc] * B_HEADS, axis=0), jnp.zeros((), BF16))
        zc = jnp.dot(wm, vbd, preferred_element_type=F32) + bfull
        bc_ref[rows, 0:B_WIDTH] = (u[rows] * zc).astype(BF16)

    first = (i % tiles_per_seq) == 0

    @pl.when(first)
    def _():
        cext_ref[0:HALO, :] = jnp.zeros((HALO, C_WIDTH), F32)

    @pl.when(jnp.logical_not(first))
    def _():
        cext_ref[0:HALO, :] = cext_ref[tm:tm + HALO, :]

    cext_ref[HALO:HALO + tm, :] = c
    ctail_ref[...] = c[tm - HALO:tm, :]
    acc = c
    sums = {}
    for j in range(1, POOL_WINDOWS[-1]):
        acc = acc + cext_ref[HALO - j:HALO - j + tm, :]
        if (j + 1) in POOL_WINDOWS:
            sums[j + 1] = acc
    wl = _pool_window_lanes((tm, C_WIDTH))
    s = sums[POOL_WINDOWS[-1]]
    for w in POOL_WINDOWS[-2::-1]:
        s = jnp.where(wl == w, sums[w], s)
    pos = (i % tiles_per_seq) * tm + lax.broadcasted_iota(jnp.int32, (tm, C_WIDTH), 0)
    cnt = jnp.minimum(pos + 1, wl).astype(F32)
    d = s / cnt - c
    co = _mm(d, _pool_weight(wpool_ref[...])) * spool_ref[...]
    bc_ref[:, B_WIDTH:B_WIDTH + C_WIDTH] = co.astype(BF16)


def _pre_sample_kernel(x_ref, cos_ref, sin_ref, g1_ref, w1_ref, w3_ref, w2_ref, gm_ref, win_ref,
                       gs_ref, sgu0_ref, state_ref, wpool_ref, spool_ref,
                       xmid_ref, q_ref, k_ref, v_ref, bc_ref, c_ref, sv_ref):
    xm = _ffn(x_ref[...], g1_ref[...], w1_ref[...], w3_ref[...], w2_ref[...])
    xmid_ref[...] = xm
    z = _mm(_rms(xm, gm_ref[...]), win_ref[...])
    q, k, v, hu, hv, c = _split_in(z)
    cos = cos_ref[...]
    sin = sin_ref[...]
    q_ref[...] = _rope(q, cos, sin) * ATTN_SCALE
    k_ref[...] = _rope(k, cos, sin)
    v_ref[...] = v
    c_ref[...] = c

    u = jax.nn.gelu(hu)
    vn = _rms(jax.nn.gelu(hv), gs_ref[...])
    sv_ref[...] = vn
    sgu0 = sgu0_ref[...]
    bc_ref[:, 0:B_WIDTH] = (u * (vn * sgu0[0:1, :] + sgu0[1:2, :])).astype(BF16)

    state = state_ref[...]
    wl3 = _pool_window_lanes(state.shape)
    r3 = lax.broadcasted_iota(jnp.int32, state.shape, 0)
    s = jnp.sum(jnp.where(r3 >= (POOL_BUF + 1) - wl3, state, 0.0), axis=0) + c
    d = s / _pool_window_lanes(c.shape).astype(F32) - c
    co = _mm(d, _pool_weight(wpool_ref[...])) * spool_ref[...]
    bc_ref[:, B_WIDTH:B_WIDTH + C_WIDTH] = co.astype(BF16)


def _lambda(lam_ref, lam_init):
    lq = lam_ref[...]
    s01 = jnp.sum(lq[0:1, :] * lq[1:2, :], axis=1, keepdims=True)
    s23 = jnp.sum(lq[2:3, :] * lq[3:4, :], axis=1, keepdims=True)
    return jnp.exp(s01) - jnp.exp(s23) + lam_init


def _attn_prompt_kernel(lam_ref, g_ref, q_ref, kt_ref, v_ref, o_ref, *, lam_init, tq, tk):
    qi = pl.program_id(2)
    per_q = tq // tk
    q = q_ref[...]
    lane = lax.broadcasted_iota(jnp.int32, q.shape, 1)
    zero = jnp.zeros((), BF16)
    q0 = jnp.where(lane < A_HEAD_DIM, q, zero)
    q1 = jnp.where(lane >= A_HEAD_DIM, q, zero)
    row = lax.broadcasted_iota(jnp.int32, (tq, tk), 0)
    col = lax.broadcasted_iota(jnp.int32, (tq, tk), 1)

    def update(s, vc, m, l, acc):
        m_new = jnp.maximum(m, jnp.max(s, axis=1, keepdims=True))
        alpha = jnp.exp(m - m_new)
        p = jnp.exp(s - m_new)
        l = alpha * l + jnp.sum(p, axis=1, keepdims=True)
        acc = alpha * acc + jnp.dot(p.astype(BF16), vc, preferred_element_type=F32)
        return m_new, l, acc

    def block(j, carry, diag):
        m0, l0, a0, m1, l1, a1 = carry
        kc = kt_ref[j]
        vc = v_ref[pl.ds(pl.multiple_of(j * tk, tk), tk), :]
        s0 = jnp.dot(q0, kc, preferred_element_type=F32)
        s1 = jnp.dot(q1, kc, preferred_element_type=F32)
        if diag is not None:
            visible = (diag * tk + col) <= row
            s0 = jnp.where(visible, s0, -jnp.inf)
            s1 = jnp.where(visible, s1, -jnp.inf)
        m0, l0, a0 = update(s0, vc, m0, l0, a0)
        m1, l1, a1 = update(s1, vc, m1, l1, a1)
        return m0, l0, a0, m1, l1, a1

    neg = jnp.full((tq, 1), -jnp.inf, F32)
    zl = jnp.zeros((tq, 1), F32)
    za = jnp.zeros((tq, HEAD_W), F32)
    carry = lax.fori_loop(0, qi * per_q, lambda j, cr: block(j, cr, None), (neg, zl, za, neg, zl, za))
    for jj in range(per_q):
        carry = block(qi * per_q + jj, carry, jj)
    m0, l0, a0, m1, l1, a1 = carry
    o = a0 / l0 - _lambda(lam_ref, lam_init) * (a1 / l1)
    o_ref[...] = (_rms(o, g_ref[...]) * (1.0 - lam_init)).astype(BF16)


def _attn_sample_kernel(pt_ref, lam_ref, g_ref, q_ref, kn_ref, vn_ref, *refs, lam_init, n_pages):
    del pt_ref
    k_refs = refs[:n_pages]
    v_refs = refs[n_pages:2 * n_pages]
    o_ref = refs[2 * n_pages]
    n_maps = 2 * A_HEADS
    q = q_ref[...]
    row = lax.broadcasted_iota(jnp.int32, (n_maps, A_WIDTH), 0)
    lane = lax.broadcasted_iota(jnp.int32, (n_maps, A_WIDTH), 1)
    owner = (row % A_HEADS) * 2 + row // A_HEADS
    qbd = jnp.where(lane // A_HEAD_DIM == owner, jnp.broadcast_to(q, (n_maps, A_WIDTH)), 0.0)
    qbd_b = qbd.astype(BF16)
    s_parts = [jnp.dot(qbd_b, k_refs[j][...].astype(BF16), preferred_element_type=F32)
               for j in range(n_pages)]
    s = jnp.concatenate(s_parts, axis=1)
    s_new = jnp.sum(qbd * kn_ref[...], axis=1, keepdims=True)
    m = jnp.maximum(jnp.max(s, axis=1, keepdims=True), s_new)
    e = jnp.exp(s - m)
    e_new = jnp.exp(s_new - m)
    l = jnp.sum(e, axis=1, keepdims=True) + e_new
    p = e / l
    p_new = e_new / l
    lam = _lambda(lam_ref, lam_init)
    a = (p[0:A_HEADS] - lam * p[A_HEADS:n_maps]).astype(BF16)
    a_new = p_new[0:A_HEADS] - lam * p_new[A_HEADS:n_maps]
    rows_per_page = PAGE_SIZE * A_HEADS
    er = lax.broadcasted_iota(jnp.int32, (PAGE_SIZE, rows_per_page), 0)
    ec = lax.broadcasted_iota(jnp.int32, (PAGE_SIZE, rows_per_page), 1)
    spread = jnp.where(ec // A_HEADS == er, 1.0, 0.0).astype(BF16)
    hr = lax.broadcasted_iota(jnp.int32, (A_HEADS, rows_per_page), 0)
    hl = lax.broadcasted_iota(jnp.int32, (A_HEADS, rows_per_page), 1)
    own_head = (hl % A_HEADS) == hr
    o = a_new * vn_ref[...]
    for j in range(n_pages):
        aj = jnp.dot(a[:, j * PAGE_SIZE:(j + 1) * PAGE_SIZE], spread, preferred_element_type=F32)
        aj = jnp.where(own_head, aj, 0.0).astype(BF16)
        o = o + jnp.dot(aj, v_refs[j][...].astype(BF16), preferred_element_type=F32)
    o_ref[...] = _rms(o, g_ref[...]) * (1.0 - lam_init)


def _post_kernel(x_ref, a_ref, bc_ref, pe_ref, wo_ref, g2_ref, w1_ref, w3_ref, w2_ref,
                 gp_ref, wpg_ref, wple_ref, gf_ref, o_ref, *, final):
    x = x_ref[...]
    x = x + _mm(a_ref[...], wo_ref[0:A_WIDTH, :]) + _mm(bc_ref[...], wo_ref[A_WIDTH:, :])
    x = _ffn(x, g2_ref[...], w1_ref[...], w3_ref[...], w2_ref[...])
    gate = jax.nn.sigmoid(_mm(_rms(x, gp_ref[...]), wpg_ref[...]))
    x = x + gate * _mm(pe_ref[...], wple_ref[...])
    if final:
        x = _rms(x, gf_ref[...])
    o_ref[...] = x


def _const_spec(shape):
    nd = len(shape)
    return pl.BlockSpec(shape, lambda *_: (0,) * nd, pipeline_mode=pl.Buffered(1))


def _row_spec(tm, width):
    return pl.BlockSpec((tm, width), lambda i: (i, 0))


def _params(n_axes):
    return pltpu.CompilerParams(dimension_semantics=("arbitrary",) * n_axes,
                                vmem_limit_bytes=VMEM_LIMIT_BYTES)


def _pre_prompt(x, cos, sin, lw, *, seq, tm):
    t, d = x.shape
    tiles_per_seq = seq // tm
    consts = [lw["g_ffn1"], lw["w1_ffn1"], lw["w3_ffn1"], lw["w2_ffn1"], lw["g_mix"], lw["w_in"],
              lw["g_sgu"], lw["wcat"], lw["bfull"], lw["wpool_t"], lw["s_pool"]]
    in_specs = [_row_spec(tm, d),
                pl.BlockSpec((tm, HEAD_W), lambda i: (i % tiles_per_seq, 0)),
                pl.BlockSpec((tm, HEAD_W), lambda i: (i % tiles_per_seq, 0))]
    in_specs += [_const_spec(c.shape) for c in consts]
    out_shape = (
        jax.ShapeDtypeStruct((t, d), F32),
        jax.ShapeDtypeStruct((t, A_WIDTH), BF16),
        jax.ShapeDtypeStruct((t // seq, tiles_per_seq, A_WIDTH, tm), BF16),
        jax.ShapeDtypeStruct((t, A_WIDTH), BF16),
        jax.ShapeDtypeStruct((t // seq, A_WIDTH, seq), F32),
        jax.ShapeDtypeStruct((t, A_WIDTH), F32),
        jax.ShapeDtypeStruct((t, B_WIDTH + C_WIDTH), BF16),
        jax.ShapeDtypeStruct((t // seq, HALO, C_WIDTH), F32),
    )
    out_specs = (
        _row_spec(tm, d), _row_spec(tm, A_WIDTH),
        pl.BlockSpec((None, None, A_WIDTH, tm), lambda i: (i // tiles_per_seq, i % tiles_per_seq, 0, 0)),
        _row_spec(tm, A_WIDTH),
        pl.BlockSpec((None, A_WIDTH, tm), lambda i: (i // tiles_per_seq, 0, i % tiles_per_seq)),
        _row_spec(tm, A_WIDTH), _row_spec(tm, B_WIDTH + C_WIDTH),
        pl.BlockSpec((None, HALO, C_WIDTH), lambda i: (i // tiles_per_seq, 0, 0)),
    )
    return pl.pallas_call(
        functools.partial(_pre_prompt_kernel, tm=tm, tiles_per_seq=tiles_per_seq),
        grid=(t // tm,),
        in_specs=in_specs,
        out_specs=out_specs,
        out_shape=out_shape,
        scratch_shapes=[pltpu.VMEM((tm + HALO, C_WIDTH), F32)],
        compiler_params=_params(1),
        name="pre_prompt",
    )(x, cos, sin, *consts)


def _pre_sample(x, cos, sin, state, lw):
    m, d = x.shape
    consts = [lw["g_ffn1"], lw["w1_ffn1"], lw["w3_ffn1"], lw["w2_ffn1"], lw["g_mix"], lw["w_in"],
              lw["g_sgu"], lw["sgu0"], state, lw["wpool_t"], lw["s_pool"]]
    args = [x, cos, sin] + consts
    out_shape = (
        jax.ShapeDtypeStruct((m, d), F32),
        jax.ShapeDtypeStruct((m, A_WIDTH), F32),
        jax.ShapeDtypeStruct((m, A_WIDTH), F32),
        jax.ShapeDtypeStruct((m, A_WIDTH), F32),
        jax.ShapeDtypeStruct((m, B_WIDTH + C_WIDTH), BF16),
        jax.ShapeDtypeStruct((m, C_WIDTH), F32),
        jax.ShapeDtypeStruct((m, B_WIDTH), F32),
    )
    return pl.pallas_call(
        _pre_sample_kernel,
        grid=(1,),
        in_specs=[_const_spec(a.shape) for a in args],
        out_specs=tuple(pl.BlockSpec(o.shape, lambda i: (0, 0)) for o in out_shape),
        out_shape=out_shape,
        compiler_params=_params(1),
        name="pre_sample",
    )(*args)


def _attn_prompt(qb, ktb, vb, lam_qk, g_subln, *, layer, batch, seq, tq):
    t = qb.shape[0]
    nq = seq // tq
    n_chunks, tk = ktb.shape[1], ktb.shape[3]
    return pl.pallas_call(
        functools.partial(_attn_prompt_kernel, lam_init=_lam_init(layer), tq=tq, tk=tk),
        grid=(batch, A_HEADS, nq),
        in_specs=[
            pl.BlockSpec(lam_qk.shape, lambda b, h, i: (0, 0)),
            pl.BlockSpec(g_subln.shape, lambda b, h, i: (0, 0)),
            pl.BlockSpec((tq, HEAD_W), lambda b, h, i: (b * nq + i, h)),
            pl.BlockSpec((None, n_chunks, HEAD_W, tk), lambda b, h, i: (b, 0, h, 0)),
            pl.BlockSpec((seq, HEAD_W), lambda b, h, i: (b, h)),
        ],
        out_specs=pl.BlockSpec((tq, HEAD_W), lambda b, h, i: (b * nq + i, h)),
        out_shape=jax.ShapeDtypeStruct((t, A_WIDTH), BF16),
        compiler_params=_params(3),
        name="attn_prompt",
    )(lam_qk, g_subln, qb, ktb, vb)


def _attn_sample(q, k_new, v_new, cache_kt, cache_v, page_table, lam_qk, g_subln, *, layer):
    nb, n_pages = page_table.shape

    def page_spec(j, shape):
        return pl.BlockSpec((None, None) + shape, lambda b, pt: (layer, pt[b, j], 0, 0))

    def row_spec(shape):
        return pl.BlockSpec((None,) + shape, lambda b, pt: (b, 0, 0))

    in_specs = [pl.BlockSpec(lam_qk.shape, lambda b, pt: (0, 0)),
                pl.BlockSpec(g_subln.shape, lambda b, pt: (0, 0)),
                row_spec((1, A_WIDTH)), row_spec((1, A_WIDTH)), row_spec((A_HEADS, HEAD_W))]
    in_specs += [page_spec(j, (A_WIDTH, PAGE_SIZE)) for j in range(n_pages)]
    in_specs += [page_spec(j, (PAGE_SIZE * A_HEADS, HEAD_W)) for j in range(n_pages)]
    grid_spec = pltpu.PrefetchScalarGridSpec(
        num_scalar_prefetch=1,
        grid=(nb,),
        in_specs=in_specs,
        out_specs=row_spec((A_HEADS, HEAD_W)),
    )
    return pl.pallas_call(
        functools.partial(_attn_sample_kernel, lam_init=_lam_init(layer), n_pages=n_pages),
        grid_spec=grid_spec,
        out_shape=jax.ShapeDtypeStruct((nb, A_HEADS, HEAD_W), F32),
        compiler_params=_params(1),
        name="attn_sample",
    )(page_table, lam_qk, g_subln, q, k_new, v_new,
      *([cache_kt] * n_pages), *([cache_v] * n_pages))


def _post(x, a, bc, pe, lw, g_final, *, tm, final):
    t, d = x.shape
    consts = [lw["w_out"], lw["g_ffn2"], lw["w1_ffn2"], lw["w3_ffn2"], lw["w2_ffn2"],
              lw["g_ple"], lw["w_pg"], lw["w_ple"], g_final]
    in_specs = [_row_spec(tm, d), _row_spec(tm, a.shape[1]), _row_spec(tm, bc.shape[1]),
                _row_spec(tm, pe.shape[1])]
    in_specs += [_const_spec(c.shape) for c in consts]
    return pl.pallas_call(
        functools.partial(_post_kernel, final=final),
        grid=(t // tm,),
        in_specs=in_specs,
        out_specs=_row_spec(tm, d),
        out_shape=jax.ShapeDtypeStruct((t, d), F32),
        compiler_params=_params(1),
        name="post",
    )(x, a, bc, pe, *consts)


def _rope_tables(pos):
    half = A_HEAD_DIM // 2
    inv = ROPE_THETA ** (-jnp.arange(half, dtype=F32) / half)
    ang = pos.astype(F32)[:, None] * inv[None, :]
    reps = HEAD_W // half
    cos = jnp.tile(jnp.cos(ang), (1, reps))
    sin = jnp.tile(jnp.sin(ang), (1, reps))
    sign = jnp.where((jnp.arange(HEAD_W) % A_HEAD_DIM) < half, -1.0, 1.0).astype(F32)
    return cos, sin * sign[None, :]


def _layer_weights(l, p):
    row = lambda a: a[l].reshape(1, -1)
    bf = lambda a: a[l].astype(BF16)
    w_sgu = p["w_sgu"][l]
    b_sgu = p["b_sgu"][l]
    per_head = B_WIDTH // B_HEADS
    return {
        "g_ffn1": row(p["g_ffn1"]), "w1_ffn1": bf(p["w1_ffn1"]), "w3_ffn1": bf(p["w3_ffn1"]),
        "w2_ffn1": bf(p["w2_ffn1"]),
        "g_mix": row(p["g_mix"]), "w_in": bf(p["w_in"]),
        "g_sgu": row(p["g_sgu"]),
        "wcat": jnp.transpose(w_sgu, (1, 0, 2)).reshape(CHUNK, B_HEADS * CHUNK),
        "bfull": jnp.repeat(b_sgu.T, per_head, axis=1),
        "sgu0": jnp.stack([jnp.repeat(w_sgu[:, 0, 0], per_head), jnp.repeat(b_sgu[:, 0], per_head)]),
        "wpool_t": jnp.tile(p["w_pool"][l].reshape(C_WIDTH, C_GROUP_DIM), (1, len(POOL_WINDOWS))),
        "s_pool": row(p["s_pool"]),
        "w_out": bf(p["w_out"]),
        "g_ffn2": row(p["g_ffn2"]), "w1_ffn2": bf(p["w1_ffn2"]), "w3_ffn2": bf(p["w3_ffn2"]),
        "w2_ffn2": bf(p["w2_ffn2"]),
        "g_ple": row(p["g_ple"]), "w_pg": bf(p["w_pg"]), "w_ple": bf(p["w_ple"]),
        "lam_qk": p["lam_qk"][l], "g_subln": row(p["g_subln"]),
    }


def kernel(x_prompt, x_sample, cache_k, cache_v, state_pool, page_table, p_prompt, p_sample, g_ffn1, w1_ffn1, w3_ffn1, w2_ffn1, g_mix, w_in, lam_qk, g_subln, g_sgu, w_sgu, b_sgu, w_pool, s_pool, w_out, g_ffn2, w1_ffn2, w3_ffn2, w2_ffn2, g_ple, w_pg, w_ple, g_final):
    params = dict(g_ffn1=g_ffn1, w1_ffn1=w1_ffn1, w3_ffn1=w3_ffn1, w2_ffn1=w2_ffn1, g_mix=g_mix,
                  w_in=w_in, lam_qk=lam_qk, g_subln=g_subln, g_sgu=g_sgu, w_sgu=w_sgu, b_sgu=b_sgu,
                  w_pool=w_pool, s_pool=s_pool, w_out=w_out, g_ffn2=g_ffn2, w1_ffn2=w1_ffn2,
                  w3_ffn2=w3_ffn2, w2_ffn2=w2_ffn2, g_ple=g_ple, w_pg=w_pg, w_ple=w_ple)
    batch, seq, d = x_prompt.shape
    nb, dec_seq, _ = x_sample.shape
    assert dec_seq == 1
    depth = cache_k.shape[0]
    n_pool = cache_k.shape[1]
    past_len = page_table.shape[1] * PAGE_SIZE
    tm = 256
    tq = 512

    cos_p, sin_p = _rope_tables(jnp.arange(seq, dtype=jnp.int32))
    cos_s, sin_s = _rope_tables(jnp.full((nb,), past_len, dtype=jnp.int32))
    ckt = jnp.transpose(cache_k, (0, 1, 3, 4, 5, 2)).reshape(depth, n_pool, A_WIDTH, PAGE_SIZE)
    cv = cache_v.reshape(depth, n_pool, PAGE_SIZE * A_HEADS, HEAD_W)
    g_fin = g_final.reshape(1, -1)

    xp = x_prompt.reshape(batch * seq, d)
    xs = x_sample.reshape(nb, d)
    kp_l, vp_l, ks_l, vs_l, pp_l, ps_l, sv_l = [], [], [], [], [], [], []
    for l in range(depth):
        lw = _layer_weights(l, params)
        final = l == depth - 1

        xmid, qb, ktb, vb, ktf, vf, bc, ctail = _pre_prompt(xp, cos_p, sin_p, lw, seq=seq, tm=tm)
        a = _attn_prompt(qb, ktb, vb, lw["lam_qk"], lw["g_subln"], layer=l, batch=batch, seq=seq, tq=tq)
        xp = _post(xmid, a, bc, p_prompt[l].reshape(batch * seq, -1), lw, g_fin, tm=tm, final=final)
        kp_l.append(jnp.transpose(ktf.reshape(batch, A_HEADS, 2, A_HEAD_DIM, seq), (0, 4, 1, 2, 3)))
        vp_l.append(vf.reshape(batch, seq, A_HEADS, HEAD_W))
        pp_l.append(ctail[:, HALO - POOL_BUF:, :])

        xmid, q, k, v, bc, c, sv = _pre_sample(xs, cos_s, sin_s, jnp.transpose(state_pool[l], (1, 0, 2)), lw)
        a = _attn_sample(q.reshape(nb, 1, A_WIDTH), k.reshape(nb, 1, A_WIDTH), v.reshape(nb, A_HEADS, HEAD_W),
                         ckt, cv, page_table, lw["lam_qk"], lw["g_subln"], layer=l)
        xs = _post(xmid, a.reshape(nb, A_WIDTH), bc, p_sample[l].reshape(nb, -1), lw, g_fin,
                   tm=nb, final=final)
        ks_l.append(k.reshape(nb, 1, A_HEADS, 2, A_HEAD_DIM))
        vs_l.append(v.reshape(nb, 1, A_HEADS, HEAD_W))
        ps_l.append(jnp.concatenate([state_pool[l][:, 1:, :], c[:, None, :]], axis=1))
        sv_l.append(sv.reshape(nb, 1, B_WIDTH))

    return (xp.reshape(batch, seq, d), xs.reshape(nb, 1, d),
            jnp.stack(kp_l), jnp.stack(vp_l), jnp.stack(ks_l), jnp.stack(vs_l),
            jnp.stack(pp_l), jnp.stack(ps_l), jnp.stack(sv_l))
```

```python
import functools
import math

import jax
import jax.numpy as jnp
from jax import lax
from jax.experimental import pallas as pl
from jax.experimental.pallas import tpu as pltpu

F32 = jnp.float32
BF16 = jnp.bfloat16

EPS = 1e-6
ROPE_THETA = 10000.0
A_HEADS = 4
A_HEAD_DIM = 64
HEAD_W = 2 * A_HEAD_DIM
A_WIDTH = A_HEADS * HEAD_W
B_WIDTH = 256
B_HEADS = 4
CHUNK = 128
C_WIDTH = 256
POOL_WINDOWS = (2, 4, 8, 16)
C_GROUP_DIM = C_WIDTH // len(POOL_WINDOWS)
HALO = 16
POOL_BUF = 15
ATTN_SCALE = A_HEAD_DIM ** -0.5
LOG2_E = math.log2(math.e)
PAGE_SIZE = 128

VMEM_LIMIT_BYTES = 56 * 1024 * 1024


def _lam_init(layer):
    return 0.8 - 0.6 * math.exp(-0.3 * layer)


def _rms(x, g):
    return x * lax.rsqrt(jnp.mean(x * x, axis=-1, keepdims=True) + EPS) * g


def _mm(a, w):
    return jnp.dot(a.astype(BF16), w, preferred_element_type=F32)


def _ffn(x, g, w1, w3, w2):
    h = _rms(x, g).astype(BF16)
    a = jnp.dot(h, w1, preferred_element_type=F32)
    b = jnp.dot(h, w3, preferred_element_type=F32)
    return x + 0.5 * _mm(a * jax.nn.sigmoid(a) * b, w2)


def _rope(z, cos, sin_signed):
    m = z.shape[0]
    lane = lax.broadcasted_iota(jnp.int32, (m, HEAD_W), 1)
    first_half = (lane % A_HEAD_DIM) < (A_HEAD_DIM // 2)
    outs = []
    for j in range(z.shape[1] // HEAD_W):
        blk = z[:, j * HEAD_W:(j + 1) * HEAD_W]
        from_right = pltpu.roll(blk, HEAD_W - A_HEAD_DIM // 2, axis=1)
        from_left = pltpu.roll(blk, A_HEAD_DIM // 2, axis=1)
        partner = jnp.where(first_half, from_right, from_left)
        outs.append(blk * cos + partner * sin_signed)
    return jnp.concatenate(outs, axis=1)


def _pool_window_lanes(shape):
    lane = lax.broadcasted_iota(jnp.int32, shape, len(shape) - 1)
    w = jnp.full(shape, POOL_WINDOWS[-1], jnp.int32)
    for gi in range(len(POOL_WINDOWS) - 2, -1, -1):
        w = jnp.where(lane < (gi + 1) * C_GROUP_DIM, POOL_WINDOWS[gi], w)
    return w


def _pool_weight(wpool_tiled):
    r = lax.broadcasted_iota(jnp.int32, wpool_tiled.shape, 0) // C_GROUP_DIM
    c = lax.broadcasted_iota(jnp.int32, wpool_tiled.shape, 1) // C_GROUP_DIM
    return jnp.where(r == c, wpool_tiled, 0.0).astype(BF16)


def _split_in(z):
    o = 0
    q = z[:, o:o + A_WIDTH]; o += A_WIDTH
    k = z[:, o:o + A_WIDTH]; o += A_WIDTH
    v = z[:, o:o + A_WIDTH]; o += A_WIDTH
    hu = z[:, o:o + B_WIDTH]; o += B_WIDTH
    hv = z[:, o:o + B_WIDTH]; o += B_WIDTH
    c = z[:, o:o + C_WIDTH]
    return q, k, v, hu, hv, c


def _pre_prompt_kernel(x_ref, cos_ref, sin_ref, g1_ref, w1_ref, w3_ref, w2_ref, gm_ref, win_ref,
                       gs_ref, wcat_ref, bfull_ref, wpool_ref, spool_ref,
                       xmid_ref, qb_ref, ktb_ref, vb_ref, ktf_ref, vf_ref, bc_ref, ctail_ref,
                       cext_ref, *, tm, tiles_per_seq):
    i = pl.program_id(0)
    xm = _ffn(x_ref[...], g1_ref[...], w1_ref[...], w3_ref[...], w2_ref[...])
    xmid_ref[...] = xm
    z = _mm(_rms(xm, gm_ref[...]), win_ref[...])
    q, k, v, hu, hv, c = _split_in(z)
    cos = cos_ref[...]
    sin = sin_ref[...]
    q = _rope(q, cos, sin)
    k = _rope(k, cos, sin)
    qb_ref[...] = (q * (ATTN_SCALE * LOG2_E)).astype(BF16)
    kt = k.T
    ktf_ref[...] = kt
    ktb_ref[...] = kt.astype(BF16)
    vb_ref[...] = v.astype(BF16)
    for h in range(A_HEADS):
        vf_ref[pl.ds(h, tm, stride=A_HEADS), :] = v[:, h * HEAD_W:(h + 1) * HEAD_W]

    u = jax.nn.gelu(hu)
    vn = _rms(jax.nn.gelu(hv), gs_ref[...])
    wr = lax.broadcasted_iota(jnp.int32, (CHUNK, B_HEADS * CHUNK), 0)
    wc = lax.broadcasted_iota(jnp.int32, (CHUNK, B_HEADS * CHUNK), 1)
    wm = jnp.where((wc % CHUNK) <= wr, wcat_ref[...], 0.0).astype(BF16)
    hr = lax.broadcasted_iota(jnp.int32, (B_HEADS * CHUNK, B_WIDTH), 0) // CHUNK
    hc = lax.broadcasted_iota(jnp.int32, (B_HEADS * CHUNK, B_WIDTH), 1) // (B_WIDTH // B_HEADS)
    head_mask = hr == hc
    bfull = bfull_ref[...]
    for ci in range(tm // CHUNK):
        rows = slice(ci * CHUNK, (ci + 1) * CHUNK)
        vc = vn[rows].astype(BF16)
        vbd = jnp.where(head_mask, jnp.concatenate([vc] * B_HEADS, axis=0), jnp.zeros((), BF16))
        zc = jnp.dot(wm, vbd, preferred_element_type=F32) + bfull
        bc_ref[rows, 0:B_WIDTH] = (u[rows] * zc).astype(BF16)

    first = (i % tiles_per_seq) == 0

    @pl.when(first)
    def _():
        cext_ref[0:HALO, :] = jnp.zeros((HALO, C_WIDTH), F32)

    @pl.when(jnp.logical_not(first))
    def _():
        cext_ref[0:HALO, :] = cext_ref[tm:tm + HALO, :]

    cext_ref[HALO:HALO + tm, :] = c
    ctail_ref[...] = c[tm - HALO:tm, :]
    acc = c
    sums = {}
    for j in range(1, POOL_WINDOWS[-1]):
        acc = acc + cext_ref[HALO - j:HALO - j + tm, :]
        if (j + 1) in POOL_WINDOWS:
            sums[j + 1] = acc
    wl = _pool_window_lanes((tm, C_WIDTH))
    s = sums[POOL_WINDOWS[-1]]
    for w in POOL_WINDOWS[-2::-1]:
        s = jnp.where(wl == w, sums[w], s)
    pos = (i % tiles_per_seq) * tm + lax.broadcasted_iota(jnp.int32, (tm, C_WIDTH), 0)
    cnt = jnp.minimum(pos + 1, wl).astype(F32)
    d = s / cnt - c
    co = _mm(d, _pool_weight(wpool_ref[...])) * spool_ref[...]
    bc_ref[:, B_WIDTH:B_WIDTH + C_WIDTH] = co.astype(BF16)


def _pre_sample_kernel(x_ref, cos_ref, sin_ref, g1_ref, w1_ref, w3_ref, w2_ref, gm_ref, win_ref,
                       gs_ref, sgu0_ref, state_ref, wpool_ref, spool_ref,
                       xmid_ref, q_ref, k_ref, v_ref, bc_ref, c_ref, sv_ref):
    xm = _ffn(x_ref[...], g1_ref[...], w1_ref[...], w3_ref[...], w2_ref[...])
    xmid_ref[...] = xm
    z = _mm(_rms(xm, gm_ref[...]), win_ref[...])
    q, k, v, hu, hv, c = _split_in(z)
    cos = cos_ref[...]
    sin = sin_ref[...]
    q_ref[...] = _rope(q, cos, sin) * ATTN_SCALE
    k_ref[...] = _rope(k, cos, sin)
    v_ref[...] = v
    c_ref[...] = c

    u = jax.nn.gelu(hu)
    vn = _rms(jax.nn.gelu(hv), gs_ref[...])
    sv_ref[...] = vn
    sgu0 = sgu0_ref[...]
    bc_ref[:, 0:B_WIDTH] = (u * (vn * sgu0[0:1, :] + sgu0[1:2, :])).astype(BF16)

    state = state_ref[...]
    wl3 = _pool_window_lanes(state.shape)
    r3 = lax.broadcasted_iota(jnp.int32, state.shape, 0)
    s = jnp.sum(jnp.where(r3 >= (POOL_BUF + 1) - wl3, state, 0.0), axis=0) + c
    d = s / _pool_window_lanes(c.shape).astype(F32) - c
    co = _mm(d, _pool_weight(wpool_ref[...])) * spool_ref[...]
    bc_ref[:, B_WIDTH:B_WIDTH + C_WIDTH] = co.astype(BF16)


def _lambda(lam_ref, lam_init):
    lq = lam_ref[...]
    s01 = jnp.sum(lq[0:1, :] * lq[1:2, :], axis=1, keepdims=True)
    s23 = jnp.sum(lq[2:3, :] * lq[3:4, :], axis=1, keepdims=True)
    return jnp.exp(s01) - jnp.exp(s23) + lam_init


def _attn_prompt_kernel(lam_ref, g_ref, q_ref, kt_ref, v_ref, o_ref, qz_ref, m_ref, l_ref, acc_ref,
                        *, lam_init, t):
    qi = pl.program_id(1)
    n_maps = 2 * A_HEADS
    lane = lax.broadcasted_iota(jnp.int32, (t, HEAD_W), 1)
    zero = jnp.zeros((), BF16)
    for h in range(A_HEADS):
        qh = q_ref[:, h * HEAD_W:(h + 1) * HEAD_W]
        qz_ref[2 * h] = jnp.where(lane < A_HEAD_DIM, qh, zero)
        qz_ref[2 * h + 1] = jnp.where(lane >= A_HEAD_DIM, qh, zero)
    m_ref[...] = jnp.full(m_ref.shape, -jnp.inf, F32)
    l_ref[...] = jnp.zeros(l_ref.shape, F32)
    acc_ref[...] = jnp.zeros(acc_ref.shape, F32)

    def block(j, masked):
        start = pl.multiple_of(j * t, t)

        def scores(i):
            h = i // 2
            s = jnp.dot(qz_ref[i], kt_ref[j, h * HEAD_W:(h + 1) * HEAD_W, :], preferred_element_type=F32)
            if masked:
                row = lax.broadcasted_iota(jnp.int32, s.shape, 0)
                col = lax.broadcasted_iota(jnp.int32, s.shape, 1)
                s = jnp.where(col <= row, s, -jnp.inf)
            return s

        ahead = 1
        queue = [scores(i) for i in range(ahead)]
        for i in range(n_maps):
            s = queue.pop(0)
            if i + ahead < n_maps:
                queue.append(scores(i + ahead))
            h = i // 2
            vc = v_ref[pl.ds(start, t), h * HEAD_W:(h + 1) * HEAD_W]
            cols = [s[:, cc * HEAD_W:(cc + 1) * HEAD_W] for cc in range(t // HEAD_W)]
            m_lane = functools.reduce(jnp.maximum, cols)
            m_old = m_ref[i]
            m_new = jnp.maximum(m_old, jnp.max(m_lane, axis=1, keepdims=True))
            alpha = jnp.exp2(m_old - m_new)
            ps = [jnp.exp2(cl - m_new) for cl in cols]
            l_ref[i] = alpha * l_ref[i] + functools.reduce(jnp.add, ps)
            p = jnp.concatenate(ps, axis=1).astype(BF16)
            acc_ref[i] = alpha * acc_ref[i] + jnp.dot(p, vc, preferred_element_type=F32)
            m_ref[i] = m_new

    def body(j, carry):
        block(j, False)
        return carry

    lax.fori_loop(0, qi, body, 0)
    block(qi, True)
    lam = _lambda(lam_ref, lam_init)
    g = g_ref[...]
    for h in range(A_HEADS):
        l0 = jnp.sum(l_ref[2 * h], axis=1, keepdims=True)
        l1 = jnp.sum(l_ref[2 * h + 1], axis=1, keepdims=True)
        o = acc_ref[2 * h] / l0 - lam * (acc_ref[2 * h + 1] / l1)
        o_ref[:, h * HEAD_W:(h + 1) * HEAD_W] = (_rms(o, g) * (1.0 - lam_init)).astype(BF16)


def _attn_sample_kernel(pt_ref, lam_ref, g_ref, q_ref, kn_ref, vn_ref, *refs, lam_init, n_pages):
    del pt_ref
    k_refs = refs[:n_pages]
    v_refs = refs[n_pages:2 * n_pages]
    o_ref = refs[2 * n_pages]
    n_maps = 2 * A_HEADS
    q = q_ref[...]
    row = lax.broadcasted_iota(jnp.int32, (n_maps, A_WIDTH), 0)
    lane = lax.broadcasted_iota(jnp.int32, (n_maps, A_WIDTH), 1)
    owner = (row % A_HEADS) * 2 + row // A_HEADS
    qbd = jnp.where(lane // A_HEAD_DIM == owner, jnp.broadcast_to(q, (n_maps, A_WIDTH)), 0.0)
    qbd_b = qbd.astype(BF16)
    kt = jnp.concatenate([k_refs[j][...].astype(BF16) for j in range(n_pages)], axis=1)
    s = jnp.dot(qbd_b, kt, preferred_element_type=F32)
    s_new = jnp.sum(qbd * kn_ref[...], axis=1, keepdims=True)
    m = jnp.maximum(jnp.max(s, axis=1, keepdims=True), s_new)
    e = jnp.exp(s - m)
    e_new = jnp.exp(s_new - m)
    l = jnp.sum(e, axis=1, keepdims=True) + e_new
    p = e / l
    p_new = e_new / l
    lam = _lambda(lam_ref, lam_init)
    a = (p[0:A_HEADS] - lam * p[A_HEADS:n_maps]).astype(BF16)
    a_new = p_new[0:A_HEADS] - lam * p_new[A_HEADS:n_maps]
    outs = []
    for h in range(A_HEADS):
        vh = jnp.concatenate(
            [v_refs[j][pl.ds(h, PAGE_SIZE, stride=A_HEADS), :].astype(BF16) for j in range(n_pages)], axis=0)
        outs.append(jnp.dot(a[h:h + 1, :], vh, preferred_element_type=F32))
    o = jnp.concatenate(outs, axis=0) + a_new * vn_ref[...]
    o_ref[...] = _rms(o, g_ref[...]) * (1.0 - lam_init)


def _post_kernel(x_ref, a_ref, bc_ref, pe_ref, wo_ref, g2_ref, w1_ref, w3_ref, w2_ref,
                 gp_ref, wpg_ref, wple_ref, gf_ref, o_ref, *, final):
    x = x_ref[...]
    x = x + _mm(a_ref[...], wo_ref[0:A_WIDTH, :]) + _mm(bc_ref[...], wo_ref[A_WIDTH:, :])
    x = _ffn(x, g2_ref[...], w1_ref[...], w3_ref[...], w2_ref[...])
    gate = jax.nn.sigmoid(_mm(_rms(x, gp_ref[...]), wpg_ref[...]))
    x = x + gate * _mm(pe_ref[...], wple_ref[...])
    if final:
        x = _rms(x, gf_ref[...])
    o_ref[...] = x


def _const_spec(shape):
    nd = len(shape)
    return pl.BlockSpec(shape, lambda *_: (0,) * nd, pipeline_mode=pl.Buffered(1))


def _row_spec(tm, width):
    return pl.BlockSpec((tm, width), lambda i: (i, 0))


def _params(n_axes):
    return pltpu.CompilerParams(dimension_semantics=("arbitrary",) * n_axes,
                                vmem_limit_bytes=VMEM_LIMIT_BYTES)


def _pre_prompt(x, cos, sin, lw, *, seq, tm, tk):
    t, d = x.shape
    tiles_per_seq = seq // tm
    per_chunk = tk // tm
    consts = [lw["g_ffn1"], lw["w1_ffn1"], lw["w3_ffn1"], lw["w2_ffn1"], lw["g_mix"], lw["w_in"],
              lw["g_sgu"], lw["wcat"], lw["bfull"], lw["wpool_t"], lw["s_pool"]]
    in_specs = [_row_spec(tm, d),
                pl.BlockSpec((tm, HEAD_W), lambda i: (i % tiles_per_seq, 0)),
                pl.BlockSpec((tm, HEAD_W), lambda i: (i % tiles_per_seq, 0))]
    in_specs += [_const_spec(c.shape) for c in consts]
    out_shape = (
        jax.ShapeDtypeStruct((t, d), F32),
        jax.ShapeDtypeStruct((t, A_WIDTH), BF16),
        jax.ShapeDtypeStruct((t // seq, seq // tk, A_WIDTH, tk), BF16),
        jax.ShapeDtypeStruct((t, A_WIDTH), BF16),
        jax.ShapeDtypeStruct((t // seq, A_WIDTH, seq), F32),
        jax.ShapeDtypeStruct((t * A_HEADS, HEAD_W), F32),
        jax.ShapeDtypeStruct((t, B_WIDTH + C_WIDTH), BF16),
        jax.ShapeDtypeStruct((t // seq, HALO, C_WIDTH), F32),
    )
    out_specs = (
        _row_spec(tm, d), _row_spec(tm, A_WIDTH),
        pl.BlockSpec((None, None, A_WIDTH, tm),
                     lambda i: (i // tiles_per_seq, (i % tiles_per_seq) // per_chunk, 0, i % per_chunk)),
        _row_spec(tm, A_WIDTH),
        pl.BlockSpec((None, A_WIDTH, tm), lambda i: (i // tiles_per_seq, 0, i % tiles_per_seq)),
        _row_spec(tm * A_HEADS, HEAD_W), _row_spec(tm, B_WIDTH + C_WIDTH),
        pl.BlockSpec((None, HALO, C_WIDTH), lambda i: (i // tiles_per_seq, 0, 0)),
    )
    return pl.pallas_call(
        functools.partial(_pre_prompt_kernel, tm=tm, tiles_per_seq=tiles_per_seq),
        grid=(t // tm,),
        in_specs=in_specs,
        out_specs=out_specs,
        out_shape=out_shape,
        scratch_shapes=[pltpu.VMEM((tm + HALO, C_WIDTH), F32)],
        compiler_params=_params(1),
        name="pre_prompt",
    )(x, cos, sin, *consts)


def _pre_sample(x, cos, sin, state, lw):
    m, d = x.shape
    consts = [lw["g_ffn1"], lw["w1_ffn1"], lw["w3_ffn1"], lw["w2_ffn1"], lw["g_mix"], lw["w_in"],
              lw["g_sgu"], lw["sgu0"], state, lw["wpool_t"], lw["s_pool"]]
    args = [x, cos, sin] + consts
    out_shape = (
        jax.ShapeDtypeStruct((m, d), F32),
        jax.ShapeDtypeStruct((m, A_WIDTH), F32),
        jax.ShapeDtypeStruct((m, A_WIDTH), F32),
        jax.ShapeDtypeStruct((m, A_WIDTH), F32),
        jax.ShapeDtypeStruct((m, B_WIDTH + C_WIDTH), BF16),
        jax.ShapeDtypeStruct((m, C_WIDTH), F32),
        jax.ShapeDtypeStruct((m, B_WIDTH), F32),
    )
    return pl.pallas_call(
        _pre_sample_kernel,
        grid=(1,),
        in_specs=[_const_spec(a.shape) for a in args],
        out_specs=tuple(pl.BlockSpec(o.shape, lambda i: (0, 0)) for o in out_shape),
        out_shape=out_shape,
        compiler_params=_params(1),
        name="pre_sample",
    )(*args)


def _attn_prompt(qb, ktb, vb, lam_qk, g_subln, *, layer, batch, seq):
    n_chunks, t = ktb.shape[1], ktb.shape[3]
    n_maps = 2 * A_HEADS
    return pl.pallas_call(
        functools.partial(_attn_prompt_kernel, lam_init=_lam_init(layer), t=t),
        grid=(batch, n_chunks),
        in_specs=[
            pl.BlockSpec(lam_qk.shape, lambda b, i: (0, 0)),
            pl.BlockSpec(g_subln.shape, lambda b, i: (0, 0)),
            pl.BlockSpec((t, A_WIDTH), lambda b, i: (b * n_chunks + i, 0)),
            pl.BlockSpec((None, n_chunks, A_WIDTH, t), lambda b, i: (b, 0, 0, 0),
                         pipeline_mode=pl.Buffered(1)),
            pl.BlockSpec((seq, A_WIDTH), lambda b, i: (b, 0), pipeline_mode=pl.Buffered(1)),
        ],
        out_specs=pl.BlockSpec((t, A_WIDTH), lambda b, i: (b * n_chunks + i, 0)),
        out_shape=jax.ShapeDtypeStruct(qb.shape, BF16),
        scratch_shapes=[pltpu.VMEM((n_maps, t, HEAD_W), BF16),
                        pltpu.VMEM((n_maps, t, HEAD_W), F32),
                        pltpu.VMEM((n_maps, t, HEAD_W), F32),
                        pltpu.VMEM((n_maps, t, HEAD_W), F32)],
        compiler_params=_params(2),
        name="attn_prompt",
    )(lam_qk, g_subln, qb, ktb, vb)


def _attn_sample(q, k_new, v_new, cache_kt, cache_v, page_table, lam_qk, g_subln, *, layer):
    nb, n_pages = page_table.shape

    def page_spec(j, shape):
        return pl.BlockSpec((None, None) + shape, lambda b, pt: (layer, pt[b, j], 0, 0))

    def row_spec(shape):
        return pl.BlockSpec((None,) + shape, lambda b, pt: (b, 0, 0))

    in_specs = [pl.BlockSpec(lam_qk.shape, lambda b, pt: (0, 0)),
                pl.BlockSpec(g_subln.shape, lambda b, pt: (0, 0)),
                row_spec((1, A_WIDTH)), row_spec((1, A_WIDTH)), row_spec((A_HEADS, HEAD_W))]
    in_specs += [page_spec(j, (A_WIDTH, PAGE_SIZE)) for j in range(n_pages)]
    in_specs += [page_spec(j, (PAGE_SIZE * A_HEADS, HEAD_W)) for j in range(n_pages)]
    grid_spec = pltpu.PrefetchScalarGridSpec(
        num_scalar_prefetch=1,
        grid=(nb,),
        in_specs=in_specs,
        out_specs=row_spec((A_HEADS, HEAD_W)),
    )
    return pl.pallas_call(
        functools.partial(_attn_sample_kernel, lam_init=_lam_init(layer), n_pages=n_pages),
        grid_spec=grid_spec,
        out_shape=jax.ShapeDtypeStruct((nb, A_HEADS, HEAD_W), F32),
        compiler_params=_params(1),
        name="attn_sample",
    )(page_table, lam_qk, g_subln, q, k_new, v_new,
      *([cache_kt] * n_pages), *([cache_v] * n_pages))


def _post(x, a, bc, pe, lw, g_final, *, tm, final):
    t, d = x.shape
    consts = [lw["w_out"], lw["g_ffn2"], lw["w1_ffn2"], lw["w3_ffn2"], lw["w2_ffn2"],
              lw["g_ple"], lw["w_pg"], lw["w_ple"], g_final]
    in_specs = [_row_spec(tm, d), _row_spec(tm, a.shape[1]), _row_spec(tm, bc.shape[1]),
                _row_spec(tm, pe.shape[1])]
    in_specs += [_const_spec(c.shape) for c in consts]
    return pl.pallas_call(
        functools.partial(_post_kernel, final=final),
        grid=(t // tm,),
        in_specs=in_specs,
        out_specs=_row_spec(tm, d),
        out_shape=jax.ShapeDtypeStruct((t, d), F32),
        compiler_params=_params(1),
        name="post",
    )(x, a, bc, pe, *consts)


def _rope_tables(pos):
    half = A_HEAD_DIM // 2
    inv = ROPE_THETA ** (-jnp.arange(half, dtype=F32) / half)
    ang = pos.astype(F32)[:, None] * inv[None, :]
    reps = HEAD_W // half
    cos = jnp.tile(jnp.cos(ang), (1, reps))
    sin = jnp.tile(jnp.sin(ang), (1, reps))
    sign = jnp.where((jnp.arange(HEAD_W) % A_HEAD_DIM) < half, -1.0, 1.0).astype(F32)
    return cos, sin * sign[None, :]


def _layer_weights(l, p):
    row = lambda a: a[l].reshape(1, -1)
    bf = lambda a: a[l].astype(BF16)
    w_sgu = p["w_sgu"][l]
    b_sgu = p["b_sgu"][l]
    per_head = B_WIDTH // B_HEADS
    return {
        "g_ffn1": row(p["g_ffn1"]), "w1_ffn1": bf(p["w1_ffn1"]), "w3_ffn1": bf(p["w3_ffn1"]),
        "w2_ffn1": bf(p["w2_ffn1"]),
        "g_mix": row(p["g_mix"]), "w_in": bf(p["w_in"]),
        "g_sgu": row(p["g_sgu"]),
        "wcat": jnp.transpose(w_sgu, (1, 0, 2)).reshape(CHUNK, B_HEADS * CHUNK),
        "bfull": jnp.repeat(b_sgu.T, per_head, axis=1),
        "sgu0": jnp.stack([jnp.repeat(w_sgu[:, 0, 0], per_head), jnp.repeat(b_sgu[:, 0], per_head)]),
        "wpool_t": jnp.tile(p["w_pool"][l].reshape(C_WIDTH, C_GROUP_DIM), (1, len(POOL_WINDOWS))),
        "s_pool": row(p["s_pool"]),
        "w_out": bf(p["w_out"]),
        "g_ffn2": row(p["g_ffn2"]), "w1_ffn2": bf(p["w1_ffn2"]), "w3_ffn2": bf(p["w3_ffn2"]),
        "w2_ffn2": bf(p["w2_ffn2"]),
        "g_ple": row(p["g_ple"]), "w_pg": bf(p["w_pg"]), "w_ple": bf(p["w_ple"]),
        "lam_qk": p["lam_qk"][l], "g_subln": row(p["g_subln"]),
    }


def kernel(x_prompt, x_sample, cache_k, cache_v, state_pool, page_table, p_prompt, p_sample, g_ffn1, w1_ffn1, w3_ffn1, w2_ffn1, g_mix, w_in, lam_qk, g_subln, g_sgu, w_sgu, b_sgu, w_pool, s_pool, w_out, g_ffn2, w1_ffn2, w3_ffn2, w2_ffn2, g_ple, w_pg, w_ple, g_final):
    params = dict(g_ffn1=g_ffn1, w1_ffn1=w1_ffn1, w3_ffn1=w3_ffn1, w2_ffn1=w2_ffn1, g_mix=g_mix,
                  w_in=w_in, lam_qk=lam_qk, g_subln=g_subln, g_sgu=g_sgu, w_sgu=w_sgu, b_sgu=b_sgu,
                  w_pool=w_pool, s_pool=s_pool, w_out=w_out, g_ffn2=g_ffn2, w1_ffn2=w1_ffn2,
                  w3_ffn2=w3_ffn2, w2_ffn2=w2_ffn2, g_ple=g_ple, w_pg=w_pg, w_ple=w_ple)
    batch, seq, d = x_prompt.shape
    nb, dec_seq, _ = x_sample.shape
    assert dec_seq == 1
    depth = cache_k.shape[0]
    n_pool = cache_k.shape[1]
    past_len = page_table.shape[1] * PAGE_SIZE
    tm = 256
    tq = 512

    cos_p, sin_p = _rope_tables(jnp.arange(seq, dtype=jnp.int32))
    cos_s, sin_s = _rope_tables(jnp.full((nb,), past_len, dtype=jnp.int32))
    ckt = jnp.transpose(cache_k, (0, 1, 3, 4, 5, 2)).reshape(depth, n_pool, A_WIDTH, PAGE_SIZE)
    cv = cache_v.reshape(depth, n_pool, PAGE_SIZE * A_HEADS, HEAD_W)
    g_fin = g_final.reshape(1, -1)

    xp = x_prompt.reshape(batch * seq, d)
    xs = x_sample.reshape(nb, d)
    kp_l, vp_l, ks_l, vs_l, pp_l, ps_l, sv_l = [], [], [], [], [], [], []
    for l in range(depth):
        lw = _layer_weights(l, params)
        final = l == depth - 1

        xmid, qb, ktb, vb, ktf, vf, bc, ctail = _pre_prompt(xp, cos_p, sin_p, lw, seq=seq, tm=tm, tk=tq)
        a = _attn_prompt(qb, ktb, vb, lw["lam_qk"], lw["g_subln"], layer=l, batch=batch, seq=seq)
        xp = _post(xmid, a, bc, p_prompt[l].reshape(batch * seq, -1), lw, g_fin, tm=tm, final=final)
        kp_l.append(jnp.transpose(ktf.reshape(batch, A_HEADS, 2, A_HEAD_DIM, seq), (0, 4, 1, 2, 3)))
        vp_l.append(vf.reshape(batch, seq, A_HEADS, HEAD_W))
        pp_l.append(ctail[:, HALO - POOL_BUF:, :])

        xmid, q, k, v, bc, c, sv = _pre_sample(xs, cos_s, sin_s, jnp.transpose(state_pool[l], (1, 0, 2)), lw)
        a = _attn_sample(q.reshape(nb, 1, A_WIDTH), k.reshape(nb, 1, A_WIDTH), v.reshape(nb, A_HEADS, HEAD_W),
                         ckt, cv, page_table, lw["lam_qk"], lw["g_subln"], layer=l)
        xs = _post(xmid, a.reshape(nb, A_WIDTH), bc, p_sample[l].reshape(nb, -1), lw, g_fin,
                   tm=nb, final=final)
        ks_l.append(k.reshape(nb, 1, A_HEADS, 2, A_HEAD_DIM))
        vs_l.append(v.reshape(nb, 1, A_HEADS, HEAD_W))
        ps_l.append(jnp.concatenate([state_pool[l][:, 1:, :], c[:, None, :]], axis=1))
        sv_l.append(sv.reshape(nb, 1, B_WIDTH))

    return (xp.reshape(batch, seq, d), xs.reshape(nb, 1, d),
            jnp.stack(kp_l), jnp.stack(vp_l), jnp.stack(ks_l), jnp.stack(vs_l),
            jnp.stack(pp_l), jnp.stack(ps_l), jnp.stack(sv_l))
```

```python
import functools
import math

import jax
import jax.numpy as jnp
from jax import lax
from jax.experimental import pallas as pl
from jax.experimental.pallas import tpu as pltpu

F32 = jnp.float32
BF16 = jnp.bfloat16

EPS = 1e-6
ROPE_THETA = 10000.0
A_HEADS = 4
A_HEAD_DIM = 64
HEAD_W = 2 * A_HEAD_DIM
A_WIDTH = A_HEADS * HEAD_W
B_WIDTH = 256
B_HEADS = 4
CHUNK = 128
C_WIDTH = 256
POOL_WINDOWS = (2, 4, 8, 16)
C_GROUP_DIM = C_WIDTH // len(POOL_WINDOWS)
HALO = 16
POOL_BUF = 15
ATTN_SCALE = A_HEAD_DIM ** -0.5
LOG2_E = math.log2(math.e)
PAGE_SIZE = 128

VMEM_LIMIT_BYTES = 56 * 1024 * 1024


def _lam_init(layer):
    return 0.8 - 0.6 * math.exp(-0.3 * layer)


def _rms(x, g):
    return x * lax.rsqrt(jnp.mean(x * x, axis=-1, keepdims=True) + EPS) * g


def _mm(a, w):
    return jnp.dot(a.astype(BF16), w, preferred_element_type=F32)


def _ffn(x, g, w1, w3, w2):
    h = _rms(x, g).astype(BF16)
    a = jnp.dot(h, w1, preferred_element_type=F32)
    b = jnp.dot(h, w3, preferred_element_type=F32)
    return x + 0.5 * _mm(a * jax.nn.sigmoid(a) * b, w2)


def _rope(z, cos, sin_signed):
    m = z.shape[0]
    lane = lax.broadcasted_iota(jnp.int32, (m, HEAD_W), 1)
    first_half = (lane % A_HEAD_DIM) < (A_HEAD_DIM // 2)
    outs = []
    for j in range(z.shape[1] // HEAD_W):
        blk = z[:, j * HEAD_W:(j + 1) * HEAD_W]
        from_right = pltpu.roll(blk, HEAD_W - A_HEAD_DIM // 2, axis=1)
        from_left = pltpu.roll(blk, A_HEAD_DIM // 2, axis=1)
        partner = jnp.where(first_half, from_right, from_left)
        outs.append(blk * cos + partner * sin_signed)
    return jnp.concatenate(outs, axis=1)


def _pool_window_lanes(shape):
    lane = lax.broadcasted_iota(jnp.int32, shape, len(shape) - 1)
    w = jnp.full(shape, POOL_WINDOWS[-1], jnp.int32)
    for gi in range(len(POOL_WINDOWS) - 2, -1, -1):
        w = jnp.where(lane < (gi + 1) * C_GROUP_DIM, POOL_WINDOWS[gi], w)
    return w


def _pool_weight(wpool_tiled):
    r = lax.broadcasted_iota(jnp.int32, wpool_tiled.shape, 0) // C_GROUP_DIM
    c = lax.broadcasted_iota(jnp.int32, wpool_tiled.shape, 1) // C_GROUP_DIM
    return jnp.where(r == c, wpool_tiled, 0.0).astype(BF16)


def _split_in(z):
    o = 0
    q = z[:, o:o + A_WIDTH]; o += A_WIDTH
    k = z[:, o:o + A_WIDTH]; o += A_WIDTH
    v = z[:, o:o + A_WIDTH]; o += A_WIDTH
    hu = z[:, o:o + B_WIDTH]; o += B_WIDTH
    hv = z[:, o:o + B_WIDTH]; o += B_WIDTH
    c = z[:, o:o + C_WIDTH]
    return q, k, v, hu, hv, c


def _pre_prompt_kernel(x_ref, cos_ref, sin_ref, g1_ref, w1_ref, w3_ref, w2_ref, gm_ref, win_ref,
                       gs_ref, wcat_ref, bfull_ref, wpool_ref, spool_ref, *rest,
                       tm, tiles_per_seq, n_aliased):
    (xmid_ref, qb_ref, ktb_ref, vb_ref, ktf_ref, vf_ref, bc_ref, ctail_ref, cext_ref) = rest[n_aliased:]
    i = pl.program_id(0)
    xm = _ffn(x_ref[...], g1_ref[...], w1_ref[...], w3_ref[...], w2_ref[...])
    xmid_ref[...] = xm
    z = _mm(_rms(xm, gm_ref[...]), win_ref[...])
    q, k, v, hu, hv, c = _split_in(z)
    cos = cos_ref[...]
    sin = sin_ref[...]
    q = _rope(q, cos, sin)
    k = _rope(k, cos, sin)
    qb_ref[...] = (q * (ATTN_SCALE * LOG2_E)).astype(BF16)
    kt = k.T
    ktf_ref[...] = kt
    ktb_ref[...] = kt.astype(BF16)
    vb_ref[...] = v.astype(BF16)
    for h in range(A_HEADS):
        vf_ref[pl.ds(h, tm, stride=A_HEADS), :] = v[:, h * HEAD_W:(h + 1) * HEAD_W]

    u = jax.nn.gelu(hu)
    vn = _rms(jax.nn.gelu(hv), gs_ref[...])
    wr = lax.broadcasted_iota(jnp.int32, (CHUNK, B_HEADS * CHUNK), 0)
    wc = lax.broadcasted_iota(jnp.int32, (CHUNK, B_HEADS * CHUNK), 1)
    wm = jnp.where((wc % CHUNK) <= wr, wcat_ref[...], 0.0).astype(BF16)
    hr = lax.broadcasted_iota(jnp.int32, (B_HEADS * CHUNK, B_WIDTH), 0) // CHUNK
    hc = lax.broadcasted_iota(jnp.int32, (B_HEADS * CHUNK, B_WIDTH), 1) // (B_WIDTH // B_HEADS)
    head_mask = hr == hc
    bfull = bfull_ref[...]
    for ci in range(tm // CHUNK):
        rows = slice(ci * CHUNK, (ci + 1) * CHUNK)
        vc = vn[rows].astype(BF16)
        vbd = jnp.where(head_mask, jnp.concatenate([vc] * B_HEADS, axis=0), jnp.zeros((), BF16))
        zc = jnp.dot(wm, vbd, preferred_element_type=F32) + bfull
        bc_ref[rows, 0:B_WIDTH] = (u[rows] * zc).astype(BF16)

    first = (i % tiles_per_seq) == 0

    @pl.when(first)
    def _():
        cext_ref[0:HALO, :] = jnp.zeros((HALO, C_WIDTH), F32)

    @pl.when(jnp.logical_not(first))
    def _():
        cext_ref[0:HALO, :] = cext_ref[tm:tm + HALO, :]

    cext_ref[HALO:HALO + tm, :] = c
    ctail_ref[...] = c[tm - HALO:tm, :]
    acc = c
    sums = {}
    for j in range(1, POOL_WINDOWS[-1]):
        acc = acc + cext_ref[HALO - j:HALO - j + tm, :]
        if (j + 1) in POOL_WINDOWS:
            sums[j + 1] = acc
    wl = _pool_window_lanes((tm, C_WIDTH))
    s = sums[POOL_WINDOWS[-1]]
    for w in POOL_WINDOWS[-2::-1]:
        s = jnp.where(wl == w, sums[w], s)
    pos = (i % tiles_per_seq) * tm + lax.broadcasted_iota(jnp.int32, (tm, C_WIDTH), 0)
    cnt = jnp.minimum(pos + 1, wl).astype(F32)
    d = s / cnt - c
    co = _mm(d, _pool_weight(wpool_ref[...])) * spool_ref[...]
    bc_ref[:, B_WIDTH:B_WIDTH + C_WIDTH] = co.astype(BF16)


def _pre_sample_kernel(x_ref, cos_ref, sin_ref, g1_ref, w1_ref, w3_ref, w2_ref, gm_ref, win_ref,
                       gs_ref, sgu0_ref, state_ref, wpool_ref, spool_ref,
                       xmid_ref, q_ref, k_ref, v_ref, bc_ref, c_ref, sv_ref):
    xm = _ffn(x_ref[...], g1_ref[...], w1_ref[...], w3_ref[...], w2_ref[...])
    xmid_ref[...] = xm
    z = _mm(_rms(xm, gm_ref[...]), win_ref[...])
    q, k, v, hu, hv, c = _split_in(z)
    cos = cos_ref[...]
    sin = sin_ref[...]
    q_ref[...] = _rope(q, cos, sin) * ATTN_SCALE
    k_ref[...] = _rope(k, cos, sin)
    v_ref[...] = v
    c_ref[...] = c

    u = jax.nn.gelu(hu)
    vn = _rms(jax.nn.gelu(hv), gs_ref[...])
    sv_ref[...] = vn
    sgu0 = sgu0_ref[...]
    bc_ref[:, 0:B_WIDTH] = (u * (vn * sgu0[0:1, :] + sgu0[1:2, :])).astype(BF16)

    state = state_ref[...]
    wl3 = _pool_window_lanes(state.shape)
    r3 = lax.broadcasted_iota(jnp.int32, state.shape, 0)
    s = jnp.sum(jnp.where(r3 >= (POOL_BUF + 1) - wl3, state, 0.0), axis=0) + c
    d = s / _pool_window_lanes(c.shape).astype(F32) - c
    co = _mm(d, _pool_weight(wpool_ref[...])) * spool_ref[...]
    bc_ref[:, B_WIDTH:B_WIDTH + C_WIDTH] = co.astype(BF16)


def _lambda(lam_ref, lam_init):
    lq = lam_ref[...]
    s01 = jnp.sum(lq[0:1, :] * lq[1:2, :], axis=1, keepdims=True)
    s23 = jnp.sum(lq[2:3, :] * lq[3:4, :], axis=1, keepdims=True)
    return jnp.exp(s01) - jnp.exp(s23) + lam_init


def _attn_prompt_kernel(lam_ref, g_ref, q_ref, kt_ref, v_ref, o_ref, qz_ref, m_ref, l_ref, acc_ref,
                        *, lam_init, t):
    qi = pl.program_id(1)
    n_maps = 2 * A_HEADS
    lane = lax.broadcasted_iota(jnp.int32, (t, HEAD_W), 1)
    zero = jnp.zeros((), BF16)
    for h in range(A_HEADS):
        qh = q_ref[:, h * HEAD_W:(h + 1) * HEAD_W]
        qz_ref[2 * h] = jnp.where(lane < A_HEAD_DIM, qh, zero)
        qz_ref[2 * h + 1] = jnp.where(lane >= A_HEAD_DIM, qh, zero)
    def block(j, diagonal):
        start = pl.multiple_of(j * t, t)

        def scores(i):
            h = i // 2
            s = jnp.dot(qz_ref[i], kt_ref[j, h * HEAD_W:(h + 1) * HEAD_W, :], preferred_element_type=F32)
            if diagonal:
                row = lax.broadcasted_iota(jnp.int32, s.shape, 0)
                col = lax.broadcasted_iota(jnp.int32, s.shape, 1)
                s = jnp.where(col <= row, s, -jnp.inf)
            return s

        s_next = scores(0)
        for i in range(n_maps):
            s = s_next
            if i + 1 < n_maps:
                s_next = scores(i + 1)
            h = i // 2
            vc = v_ref[pl.ds(start, t), h * HEAD_W:(h + 1) * HEAD_W]
            cols = [s[:, cc * HEAD_W:(cc + 1) * HEAD_W] for cc in range(t // HEAD_W)]
            m_new = jnp.broadcast_to(jnp.max(functools.reduce(jnp.maximum, cols), axis=1, keepdims=True),
                                     (t, HEAD_W))
            if not diagonal:
                m_old = m_ref[i]
                m_new = jnp.maximum(m_old, m_new)
                alpha = jnp.exp2(m_old - m_new)
            ps = [jnp.exp2((cl - m_new).astype(BF16)) for cl in cols]
            l_new = functools.reduce(jnp.add, ps).astype(F32)
            pv = jnp.dot(jnp.concatenate(ps, axis=1), vc, preferred_element_type=F32)
            if not diagonal:
                l_new = alpha * l_ref[i] + l_new
                pv = alpha * acc_ref[i] + pv
            l_ref[i] = l_new
            acc_ref[i] = pv
            m_ref[i] = m_new

    def body(j, carry):
        block(j, False)
        return carry

    block(qi, True)
    lax.fori_loop(0, qi, body, 0)
    lam = _lambda(lam_ref, lam_init)
    g = g_ref[...]
    for h in range(A_HEADS):
        l0 = jnp.sum(l_ref[2 * h], axis=1, keepdims=True)
        l1 = jnp.sum(l_ref[2 * h + 1], axis=1, keepdims=True)
        o = acc_ref[2 * h] / l0 - lam * (acc_ref[2 * h + 1] / l1)
        o_ref[:, h * HEAD_W:(h + 1) * HEAD_W] = (_rms(o, g) * (1.0 - lam_init)).astype(BF16)


def _attn_sample_kernel(pt_ref, lam_ref, g_ref, q_ref, kn_ref, vn_ref, *refs, lam_init, n_pages):
    del pt_ref
    k_refs = refs[:n_pages]
    v_refs = refs[n_pages:2 * n_pages]
    o_ref = refs[2 * n_pages]
    n_maps = 2 * A_HEADS
    q = q_ref[...]
    row = lax.broadcasted_iota(jnp.int32, (n_maps, A_WIDTH), 0)
    lane = lax.broadcasted_iota(jnp.int32, (n_maps, A_WIDTH), 1)
    owner = (row % A_HEADS) * 2 + row // A_HEADS
    qbd = jnp.where(lane // A_HEAD_DIM == owner, jnp.broadcast_to(q, (n_maps, A_WIDTH)), 0.0)
    qbd_b = qbd.astype(BF16)
    kt = jnp.concatenate([k_refs[j][...].astype(BF16) for j in range(n_pages)], axis=1)
    s = jnp.dot(qbd_b, kt, preferred_element_type=F32)
    s_new = jnp.sum(qbd * kn_ref[...], axis=1, keepdims=True)
    m = jnp.maximum(jnp.max(s, axis=1, keepdims=True), s_new)
    e = jnp.exp(s - m)
    e_new = jnp.exp(s_new - m)
    l = jnp.sum(e, axis=1, keepdims=True) + e_new
    p = e / l
    p_new = e_new / l
    lam = _lambda(lam_ref, lam_init)
    a = (p[0:A_HEADS] - lam * p[A_HEADS:n_maps]).astype(BF16)
    a_new = p_new[0:A_HEADS] - lam * p_new[A_HEADS:n_maps]
    outs = []
    for h in range(A_HEADS):
        vh = jnp.concatenate(
            [v_refs[j][pl.ds(h, PAGE_SIZE, stride=A_HEADS), :].astype(BF16) for j in range(n_pages)], axis=0)
        outs.append(jnp.dot(a[h:h + 1, :], vh, preferred_element_type=F32))
    o = jnp.concatenate(outs, axis=0) + a_new * vn_ref[...]
    o_ref[...] = _rms(o, g_ref[...]) * (1.0 - lam_init)


def _post_kernel(x_ref, a_ref, bc_ref, pe_ref, wo_ref, g2_ref, w1_ref, w3_ref, w2_ref,
                 gp_ref, wpg_ref, wple_ref, gf_ref, o_ref, *, final):
    x = x_ref[...]
    x = x + _mm(a_ref[...], wo_ref[0:A_WIDTH, :]) + _mm(bc_ref[...], wo_ref[A_WIDTH:, :])
    x = _ffn(x, g2_ref[...], w1_ref[...], w3_ref[...], w2_ref[...])
    gate = jax.nn.sigmoid(_mm(_rms(x, gp_ref[...]), wpg_ref[...]))
    x = x + gate * _mm(pe_ref[...], wple_ref[...])
    if final:
        x = _rms(x, gf_ref[...])
    o_ref[...] = x


def _const_spec(shape):
    nd = len(shape)
    return pl.BlockSpec(shape, lambda *_: (0,) * nd, pipeline_mode=pl.Buffered(1))


def _layer_spec(stacked, layer):
    rest = stacked.shape[1:]
    return pl.BlockSpec((None,) + rest, lambda *_: (layer,) + (0,) * len(rest),
                        pipeline_mode=pl.Buffered(1))


def _row_spec(tm, width):
    return pl.BlockSpec((tm, width), lambda i: (i, 0))


def _params(n_axes):
    return pltpu.CompilerParams(dimension_semantics=("arbitrary",) * n_axes,
                                vmem_limit_bytes=VMEM_LIMIT_BYTES)


def _pre_prompt(x, cos, sin, w, layer, kv_out, *, depth, seq, tm, tk):
    t, d = x.shape
    tiles_per_seq = seq // tm
    per_chunk = tk // tm
    n_tiles = t // tm
    consts = [w[k] for k in ("g_ffn1", "w1_ffn1", "w3_ffn1", "w2_ffn1", "g_mix", "w_in",
                             "g_sgu", "wcat", "bfull", "wpool_t", "s_pool")]
    args = [x, cos, sin] + consts
    in_specs = [_row_spec(tm, d),
                pl.BlockSpec((tm, HEAD_W), lambda i: (i % tiles_per_seq, 0)),
                pl.BlockSpec((tm, HEAD_W), lambda i: (i % tiles_per_seq, 0))]
    in_specs += [_layer_spec(c, layer) for c in consts]
    aliases = {}
    if kv_out is not None:
        aliases = {len(args): 4, len(args) + 1: 5}
        args += list(kv_out)
        in_specs += [pl.BlockSpec(memory_space=pl.ANY)] * 2
    out_shape = (
        jax.ShapeDtypeStruct((t, d), F32),
        jax.ShapeDtypeStruct((t, A_WIDTH), BF16),
        jax.ShapeDtypeStruct((t // seq, seq // tk, A_WIDTH, tk), BF16),
        jax.ShapeDtypeStruct((t, A_WIDTH), BF16),
        jax.ShapeDtypeStruct((depth, t // seq, A_WIDTH, seq), F32),
        jax.ShapeDtypeStruct((depth * t * A_HEADS, HEAD_W), F32),
        jax.ShapeDtypeStruct((t, B_WIDTH + C_WIDTH), BF16),
        jax.ShapeDtypeStruct((t // seq, HALO, C_WIDTH), F32),
    )
    out_specs = (
        _row_spec(tm, d), _row_spec(tm, A_WIDTH),
        pl.BlockSpec((None, None, A_WIDTH, tm),
                     lambda i: (i // tiles_per_seq, (i % tiles_per_seq) // per_chunk, 0, i % per_chunk)),
        _row_spec(tm, A_WIDTH),
        pl.BlockSpec((None, None, A_WIDTH, tm), lambda i: (layer, i // tiles_per_seq, 0, i % tiles_per_seq)),
        pl.BlockSpec((tm * A_HEADS, HEAD_W), lambda i: (layer * n_tiles + i, 0)),
        _row_spec(tm, B_WIDTH + C_WIDTH),
        pl.BlockSpec((None, HALO, C_WIDTH), lambda i: (i // tiles_per_seq, 0, 0)),
    )
    return pl.pallas_call(
        functools.partial(_pre_prompt_kernel, tm=tm, tiles_per_seq=tiles_per_seq, n_aliased=len(aliases)),
        grid=(n_tiles,),
        in_specs=in_specs,
        out_specs=out_specs,
        out_shape=out_shape,
        input_output_aliases=aliases,
        scratch_shapes=[pltpu.VMEM((tm + HALO, C_WIDTH), F32)],
        compiler_params=_params(1),
        name="pre_prompt",
    )(*args)


def _pre_sample(x, cos, sin, state, w, layer):
    m, d = x.shape
    consts = [w[k] for k in ("g_ffn1", "w1_ffn1", "w3_ffn1", "w2_ffn1", "g_mix", "w_in", "g_sgu", "sgu0")]
    consts += [state, w["wpool_t"], w["s_pool"]]
    args = [x, cos, sin] + consts
    out_shape = (
        jax.ShapeDtypeStruct((m, d), F32),
        jax.ShapeDtypeStruct((m, A_WIDTH), F32),
        jax.ShapeDtypeStruct((m, A_WIDTH), F32),
        jax.ShapeDtypeStruct((m, A_WIDTH), F32),
        jax.ShapeDtypeStruct((m, B_WIDTH + C_WIDTH), BF16),
        jax.ShapeDtypeStruct((m, C_WIDTH), F32),
        jax.ShapeDtypeStruct((m, B_WIDTH), F32),
    )
    return pl.pallas_call(
        _pre_sample_kernel,
        grid=(1,),
        in_specs=[_const_spec(a.shape) for a in args[:3]] + [_layer_spec(c, layer) for c in consts],
        out_specs=tuple(pl.BlockSpec(o.shape, lambda i: (0, 0)) for o in out_shape),
        out_shape=out_shape,
        compiler_params=_params(1),
        name="pre_sample",
    )(*args)


def _attn_prompt(qb, ktb, vb, lam_qk, g_subln, *, layer, batch, seq):
    n_chunks, t = ktb.shape[1], ktb.shape[3]
    n_maps = 2 * A_HEADS
    return pl.pallas_call(
        functools.partial(_attn_prompt_kernel, lam_init=_lam_init(layer), t=t),
        grid=(batch, n_chunks),
        in_specs=[
            pl.BlockSpec((None,) + lam_qk.shape[1:], lambda b, i: (layer, 0, 0)),
            pl.BlockSpec((None,) + g_subln.shape[1:], lambda b, i: (layer, 0, 0)),
            pl.BlockSpec((t, A_WIDTH), lambda b, i: (b * n_chunks + i, 0)),
            pl.BlockSpec((None, n_chunks, A_WIDTH, t), lambda b, i: (b, 0, 0, 0),
                         pipeline_mode=pl.Buffered(1)),
            pl.BlockSpec((seq, A_WIDTH), lambda b, i: (b, 0), pipeline_mode=pl.Buffered(1)),
        ],
        out_specs=pl.BlockSpec((t, A_WIDTH), lambda b, i: (b * n_chunks + i, 0)),
        out_shape=jax.ShapeDtypeStruct(qb.shape, BF16),
        scratch_shapes=[pltpu.VMEM((n_maps, t, HEAD_W), BF16),
                        pltpu.VMEM((n_maps, t, HEAD_W), F32),
                        pltpu.VMEM((n_maps, t, HEAD_W), F32),
                        pltpu.VMEM((n_maps, t, HEAD_W), F32)],
        compiler_params=_params(2),
        name="attn_prompt",
    )(lam_qk, g_subln, qb, ktb, vb)


def _attn_sample(q, k_new, v_new, cache_kt, cache_v, page_table, lam_qk, g_subln, *, layer):
    nb, n_pages = page_table.shape

    def page_spec(j, shape):
        return pl.BlockSpec((None, None) + shape, lambda b, pt: (layer, pt[b, j], 0, 0))

    def row_spec(shape):
        return pl.BlockSpec((None,) + shape, lambda b, pt: (b, 0, 0))

    in_specs = [pl.BlockSpec((None,) + lam_qk.shape[1:], lambda b, pt: (layer, 0, 0)),
                pl.BlockSpec((None,) + g_subln.shape[1:], lambda b, pt: (layer, 0, 0)),
                row_spec((1, A_WIDTH)), row_spec((1, A_WIDTH)), row_spec((A_HEADS, HEAD_W))]
    in_specs += [page_spec(j, (A_WIDTH, PAGE_SIZE)) for j in range(n_pages)]
    in_specs += [page_spec(j, (PAGE_SIZE * A_HEADS, HEAD_W)) for j in range(n_pages)]
    grid_spec = pltpu.PrefetchScalarGridSpec(
        num_scalar_prefetch=1,
        grid=(nb,),
        in_specs=in_specs,
        out_specs=row_spec((A_HEADS, HEAD_W)),
    )
    return pl.pallas_call(
        functools.partial(_attn_sample_kernel, lam_init=_lam_init(layer), n_pages=n_pages),
        grid_spec=grid_spec,
        out_shape=jax.ShapeDtypeStruct((nb, A_HEADS, HEAD_W), F32),
        compiler_params=_params(1),
        name="attn_sample",
    )(page_table, lam_qk, g_subln, q, k_new, v_new,
      *([cache_kt] * n_pages), *([cache_v] * n_pages))


def _post(x, a, bc, pe, w, layer, g_final, *, tm, final):
    t, d = x.shape
    consts = [w[k] for k in ("w_out", "g_ffn2", "w1_ffn2", "w3_ffn2", "w2_ffn2", "g_ple", "w_pg", "w_ple")]
    in_specs = [_row_spec(tm, d), _row_spec(tm, a.shape[1]), _row_spec(tm, bc.shape[1]),
                pl.BlockSpec((None, tm, pe.shape[2]), lambda i: (layer, i, 0))]
    in_specs += [_layer_spec(c, layer) for c in consts] + [_const_spec(g_final.shape)]
    return pl.pallas_call(
        functools.partial(_post_kernel, final=final),
        grid=(t // tm,),
        in_specs=in_specs,
        out_specs=_row_spec(tm, d),
        out_shape=jax.ShapeDtypeStruct((t, d), F32),
        compiler_params=_params(1),
        name="post",
    )(x, a, bc, pe, *consts, g_final)


def _rope_tables(pos):
    half = A_HEAD_DIM // 2
    inv = ROPE_THETA ** (-jnp.arange(half, dtype=F32) / half)
    ang = pos.astype(F32)[:, None] * inv[None, :]
    reps = HEAD_W // half
    cos = jnp.tile(jnp.cos(ang), (1, reps))
    sin = jnp.tile(jnp.sin(ang), (1, reps))
    sign = jnp.where((jnp.arange(HEAD_W) % A_HEAD_DIM) < half, -1.0, 1.0).astype(F32)
    return cos, sin * sign[None, :]


def _prepare_weights(p):
    depth = p["g_mix"].shape[0]
    row = lambda a: a.reshape(depth, 1, -1)
    bf = lambda a: a.astype(BF16)
    w_sgu = p["w_sgu"]
    b_sgu = p["b_sgu"]
    per_head = B_WIDTH // B_HEADS
    return {
        "g_ffn1": row(p["g_ffn1"]), "w1_ffn1": bf(p["w1_ffn1"]), "w3_ffn1": bf(p["w3_ffn1"]),
        "w2_ffn1": bf(p["w2_ffn1"]),
        "g_mix": row(p["g_mix"]), "w_in": bf(p["w_in"]),
        "g_sgu": row(p["g_sgu"]),
        "wcat": jnp.transpose(w_sgu, (0, 2, 1, 3)).reshape(depth, CHUNK, B_HEADS * CHUNK),
        "bfull": jnp.repeat(jnp.transpose(b_sgu, (0, 2, 1)), per_head, axis=2),
        "sgu0": jnp.stack([jnp.repeat(w_sgu[:, :, 0, 0], per_head, axis=1),
                           jnp.repeat(b_sgu[:, :, 0], per_head, axis=1)], axis=1),
        "wpool_t": jnp.tile(p["w_pool"].reshape(depth, C_WIDTH, C_GROUP_DIM), (1, 1, len(POOL_WINDOWS))),
        "s_pool": row(p["s_pool"]),
        "w_out": bf(p["w_out"]),
        "g_ffn2": row(p["g_ffn2"]), "w1_ffn2": bf(p["w1_ffn2"]), "w3_ffn2": bf(p["w3_ffn2"]),
        "w2_ffn2": bf(p["w2_ffn2"]),
        "g_ple": row(p["g_ple"]), "w_pg": bf(p["w_pg"]), "w_ple": bf(p["w_ple"]),
        "lam_qk": p["lam_qk"], "g_subln": row(p["g_subln"]),
    }


def kernel(x_prompt, x_sample, cache_k, cache_v, state_pool, page_table, p_prompt, p_sample, g_ffn1, w1_ffn1, w3_ffn1, w2_ffn1, g_mix, w_in, lam_qk, g_subln, g_sgu, w_sgu, b_sgu, w_pool, s_pool, w_out, g_ffn2, w1_ffn2, w3_ffn2, w2_ffn2, g_ple, w_pg, w_ple, g_final):
    params = dict(g_ffn1=g_ffn1, w1_ffn1=w1_ffn1, w3_ffn1=w3_ffn1, w2_ffn1=w2_ffn1, g_mix=g_mix,
                  w_in=w_in, lam_qk=lam_qk, g_subln=g_subln, g_sgu=g_sgu, w_sgu=w_sgu, b_sgu=b_sgu,
                  w_pool=w_pool, s_pool=s_pool, w_out=w_out, g_ffn2=g_ffn2, w1_ffn2=w1_ffn2,
                  w3_ffn2=w3_ffn2, w2_ffn2=w2_ffn2, g_ple=g_ple, w_pg=w_pg, w_ple=w_ple)
    batch, seq, d = x_prompt.shape
    nb, dec_seq, _ = x_sample.shape
    assert dec_seq == 1
    depth = cache_k.shape[0]
    n_pool = cache_k.shape[1]
    past_len = page_table.shape[1] * PAGE_SIZE
    tm = 256
    tq = 512

    cos_p, sin_p = _rope_tables(jnp.arange(seq, dtype=jnp.int32))
    cos_s, sin_s = _rope_tables(jnp.full((nb,), past_len, dtype=jnp.int32))
    ckt = jnp.transpose(cache_k, (0, 1, 3, 4, 5, 2)).reshape(depth, n_pool, A_WIDTH, PAGE_SIZE)
    cv = cache_v.reshape(depth, n_pool, PAGE_SIZE * A_HEADS, HEAD_W)
    g_fin = g_final.reshape(1, -1)
    w = _prepare_weights(params)
    state_t = jnp.transpose(state_pool, (0, 2, 1, 3))
    pe_p = p_prompt.reshape(depth, batch * seq, -1)
    pe_s = p_sample.reshape(depth, nb, -1)

    xp = x_prompt.reshape(batch * seq, d)
    xs = x_sample.reshape(nb, d)
    kv_out = None
    ks_l, vs_l, pp_l, ps_l, sv_l = [], [], [], [], []
    for l in range(depth):
        final = l == depth - 1

        xmid, qb, ktb, vb, ktf, vf, bc, ctail = _pre_prompt(xp, cos_p, sin_p, w, l, kv_out,
                                                            depth=depth, seq=seq, tm=tm, tk=tq)
        kv_out = (ktf, vf)
        a = _attn_prompt(qb, ktb, vb, w["lam_qk"], w["g_subln"], layer=l, batch=batch, seq=seq)
        xp = _post(xmid, a, bc, pe_p, w, l, g_fin, tm=tm, final=final)
        pp_l.append(ctail[:, HALO - POOL_BUF:, :])

        xmid, q, k, v, bc, c, sv = _pre_sample(xs, cos_s, sin_s, state_t, w, l)
        a = _attn_sample(q.reshape(nb, 1, A_WIDTH), k.reshape(nb, 1, A_WIDTH), v.reshape(nb, A_HEADS, HEAD_W),
                         ckt, cv, page_table, w["lam_qk"], w["g_subln"], layer=l)
        xs = _post(xmid, a.reshape(nb, A_WIDTH), bc, pe_s, w, l, g_fin, tm=nb, final=final)
        ks_l.append(k.reshape(nb, 1, A_HEADS, 2, A_HEAD_DIM))
        vs_l.append(v.reshape(nb, 1, A_HEADS, HEAD_W))
        ps_l.append(jnp.concatenate([state_pool[l][:, 1:, :], c[:, None, :]], axis=1))
        sv_l.append(sv.reshape(nb, 1, B_WIDTH))

    ktf, vf = kv_out
    k_prompt = jnp.transpose(ktf.reshape(depth, batch, A_HEADS, 2, A_HEAD_DIM, seq), (0, 1, 5, 2, 3, 4))
    v_prompt = vf.reshape(depth, batch, seq, A_HEADS, HEAD_W)
    return (xp.reshape(batch, seq, d), xs.reshape(nb, 1, d),
            k_prompt, v_prompt, jnp.stack(ks_l), jnp.stack(vs_l),
            jnp.stack(pp_l), jnp.stack(ps_l), jnp.stack(sv_l))
```

```python
import functools
import math

import jax
import jax.numpy as jnp
from jax import lax
from jax.experimental import pallas as pl
from jax.experimental.pallas import tpu as pltpu

F32 = jnp.float32
BF16 = jnp.bfloat16

EPS = 1e-6
ROPE_THETA = 10000.0
A_HEADS = 4
A_HEAD_DIM = 64
HEAD_W = 2 * A_HEAD_DIM
A_WIDTH = A_HEADS * HEAD_W
B_WIDTH = 256
B_HEADS = 4
CHUNK = 128
C_WIDTH = 256
POOL_WINDOWS = (2, 4, 8, 16)
C_GROUP_DIM = C_WIDTH // len(POOL_WINDOWS)
HALO = 16
POOL_BUF = 15
ATTN_SCALE = A_HEAD_DIM ** -0.5
LOG2_E = math.log2(math.e)
PAGE_SIZE = 128

VMEM_LIMIT_BYTES = 56 * 1024 * 1024


def _lam_init(layer):
    return 0.8 - 0.6 * math.exp(-0.3 * layer)


def _rms(x, g):
    return x * lax.rsqrt(jnp.mean(x * x, axis=-1, keepdims=True) + EPS) * g


def _mm(a, w):
    return jnp.dot(a.astype(BF16), w, preferred_element_type=F32)


def _normed_mm(x, wg):
    return _mm(x, wg) * lax.rsqrt(jnp.mean(x * x, axis=-1, keepdims=True) + EPS)


def _ffn(x, w1g, w3g, w2):
    xb = x.astype(BF16)
    r = lax.rsqrt(jnp.mean(x * x, axis=-1, keepdims=True) + EPS)
    a = jnp.dot(xb, w1g, preferred_element_type=F32) * r
    b = jnp.dot(xb, w3g, preferred_element_type=F32) * r
    return x + 0.5 * _mm(a * jax.nn.sigmoid(a) * b, w2)


def _rope(z, cos, sin_signed):
    m = z.shape[0]
    lane = lax.broadcasted_iota(jnp.int32, (m, HEAD_W), 1)
    first_half = (lane % A_HEAD_DIM) < (A_HEAD_DIM // 2)
    outs = []
    for j in range(z.shape[1] // HEAD_W):
        blk = z[:, j * HEAD_W:(j + 1) * HEAD_W]
        from_right = pltpu.roll(blk, HEAD_W - A_HEAD_DIM // 2, axis=1)
        from_left = pltpu.roll(blk, A_HEAD_DIM // 2, axis=1)
        partner = jnp.where(first_half, from_right, from_left)
        outs.append(blk * cos + partner * sin_signed)
    return jnp.concatenate(outs, axis=1)


def _pool_window_lanes(shape):
    lane = lax.broadcasted_iota(jnp.int32, shape, len(shape) - 1)
    w = jnp.full(shape, POOL_WINDOWS[-1], jnp.int32)
    for gi in range(len(POOL_WINDOWS) - 2, -1, -1):
        w = jnp.where(lane < (gi + 1) * C_GROUP_DIM, POOL_WINDOWS[gi], w)
    return w


def _pool_weight(wpool_tiled):
    r = lax.broadcasted_iota(jnp.int32, wpool_tiled.shape, 0) // C_GROUP_DIM
    c = lax.broadcasted_iota(jnp.int32, wpool_tiled.shape, 1) // C_GROUP_DIM
    return jnp.where(r == c, wpool_tiled, 0.0).astype(BF16)


def _split_in(z):
    o = 0
    q = z[:, o:o + A_WIDTH]; o += A_WIDTH
    k = z[:, o:o + A_WIDTH]; o += A_WIDTH
    v = z[:, o:o + A_WIDTH]; o += A_WIDTH
    hu = z[:, o:o + B_WIDTH]; o += B_WIDTH
    hv = z[:, o:o + B_WIDTH]; o += B_WIDTH
    c = z[:, o:o + C_WIDTH]
    return q, k, v, hu, hv, c


def _pre_prompt_kernel(x_ref, cos_ref, sin_ref, w1_ref, w3_ref, w2_ref, win_ref,
                       gs_ref, wcat_ref, bfull_ref, wpool_ref, spool_ref, *rest,
                       tm, tiles_per_seq, n_aliased):
    (xmid_ref, qb_ref, ktb_ref, vb_ref, ktf_ref, vf_ref, bc_ref, ctail_ref, z_ref, cext_ref) = rest[n_aliased:]
    i = pl.program_id(0)

    @pl.when(i == 0)
    def _():
        z_ref[...] = jnp.zeros(z_ref.shape, F32)
        cext_ref[...] = jnp.zeros(cext_ref.shape, F32)

    tile = jnp.maximum(i - 1, 0) % tiles_per_seq
    q, k, v, hu, hv, c = _split_in(z_ref[...])
    cos = cos_ref[...]
    sin = sin_ref[...]
    q = _rope(q, cos, sin)
    k = _rope(k, cos, sin)
    qb_ref[...] = (q * (ATTN_SCALE * LOG2_E)).astype(BF16)
    kt = k.T
    ktf_ref[...] = kt
    ktb_ref[...] = kt.astype(BF16)
    vb_ref[...] = v.astype(BF16)
    for h in range(A_HEADS):
        vf_ref[pl.ds(h, tm, stride=A_HEADS), :] = v[:, h * HEAD_W:(h + 1) * HEAD_W]
    u = jax.nn.gelu(hu)
    vn = _rms(jax.nn.gelu(hv), gs_ref[...])

    cext_ref[0:HALO, :] = jnp.where(tile == 0, 0.0, cext_ref[tm:tm + HALO, :])
    cext_ref[HALO:HALO + tm, :] = c
    ctail_ref[...] = c[tm - HALO:tm, :]
    acc = c
    sums = {}
    for j in range(1, POOL_WINDOWS[-1]):
        acc = acc + cext_ref[HALO - j:HALO - j + tm, :]
        if (j + 1) in POOL_WINDOWS:
            sums[j + 1] = acc
    wl = _pool_window_lanes((tm, C_WIDTH))
    s = sums[POOL_WINDOWS[-1]]
    for w in POOL_WINDOWS[-2::-1]:
        s = jnp.where(wl == w, sums[w], s)
    pos = tile * tm + lax.broadcasted_iota(jnp.int32, (tm, C_WIDTH), 0)
    cnt = jnp.minimum(pos + 1, wl).astype(F32)
    d = s / cnt - c

    xm = _ffn(x_ref[...], w1_ref[...], w3_ref[...], w2_ref[...])
    xmid_ref[...] = xm
    z_ref[...] = _normed_mm(xm, win_ref[...])

    wr = lax.broadcasted_iota(jnp.int32, (CHUNK, B_HEADS * CHUNK), 0)
    wc = lax.broadcasted_iota(jnp.int32, (CHUNK, B_HEADS * CHUNK), 1)
    wm = jnp.where((wc % CHUNK) <= wr, wcat_ref[...], 0.0).astype(BF16)
    hr = lax.broadcasted_iota(jnp.int32, (B_HEADS * CHUNK, B_WIDTH), 0) // CHUNK
    hc = lax.broadcasted_iota(jnp.int32, (B_HEADS * CHUNK, B_WIDTH), 1) // (B_WIDTH // B_HEADS)
    head_mask = hr == hc
    bfull = bfull_ref[...]
    for ci in range(tm // CHUNK):
        rows = slice(ci * CHUNK, (ci + 1) * CHUNK)
        vc = vn[rows].astype(BF16)
        vbd = jnp.where(head_mask, jnp.concatenate([vc] * B_HEADS, axis=0), jnp.zeros((), BF16))
        zc = jnp.dot(wm, vbd, preferred_element_type=F32) + bfull
        bc_ref[rows, 0:B_WIDTH] = (u[rows] * zc).astype(BF16)
    co = _mm(d, _pool_weight(wpool_ref[...])) * spool_ref[...]
    bc_ref[:, B_WIDTH:B_WIDTH + C_WIDTH] = co.astype(BF16)


def _pre_sample_kernel(x_ref, cos_ref, sin_ref, w1_ref, w3_ref, w2_ref, win_ref,
                       gs_ref, sgu0_ref, state_ref, wpool_ref, spool_ref,
                       xmid_ref, q_ref, k_ref, v_ref, bc_ref, c_ref, sv_ref):
    xm = _ffn(x_ref[...], w1_ref[...], w3_ref[...], w2_ref[...])
    xmid_ref[...] = xm
    z = _normed_mm(xm, win_ref[...])
    q, k, v, hu, hv, c = _split_in(z)
    cos = cos_ref[...]
    sin = sin_ref[...]
    q_ref[...] = _rope(q, cos, sin) * ATTN_SCALE
    k_ref[...] = _rope(k, cos, sin)
    v_ref[...] = v
    c_ref[...] = c

    u = jax.nn.gelu(hu)
    vn = _rms(jax.nn.gelu(hv), gs_ref[...])
    sv_ref[...] = vn
    sgu0 = sgu0_ref[...]
    bc_ref[:, 0:B_WIDTH] = (u * (vn * sgu0[0:1, :] + sgu0[1:2, :])).astype(BF16)

    state = state_ref[...]
    wl3 = _pool_window_lanes(state.shape)
    r3 = lax.broadcasted_iota(jnp.int32, state.shape, 0)
    s = jnp.sum(jnp.where(r3 >= (POOL_BUF + 1) - wl3, state, 0.0), axis=0) + c
    d = s / _pool_window_lanes(c.shape).astype(F32) - c
    co = _mm(d, _pool_weight(wpool_ref[...])) * spool_ref[...]
    bc_ref[:, B_WIDTH:B_WIDTH + C_WIDTH] = co.astype(BF16)


def _lambda(lam_ref, lam_init):
    lq = lam_ref[...]
    s01 = jnp.sum(lq[0:1, :] * lq[1:2, :], axis=1, keepdims=True)
    s23 = jnp.sum(lq[2:3, :] * lq[3:4, :], axis=1, keepdims=True)
    return jnp.exp(s01) - jnp.exp(s23) + lam_init


def _attn_prompt_kernel(lam_ref, g_ref, q_ref, kt_ref, v_ref, o_ref, qz_ref, m_ref, l_ref, acc_ref,
                        *, lam_init, t):
    qi = pl.program_id(1)
    n_maps = 2 * A_HEADS
    lane = lax.broadcasted_iota(jnp.int32, (t, HEAD_W), 1)
    zero = jnp.zeros((), BF16)
    for h in range(A_HEADS):
        qh = q_ref[:, h * HEAD_W:(h + 1) * HEAD_W]
        qz_ref[2 * h] = jnp.where(lane < A_HEAD_DIM, qh, zero)
        qz_ref[2 * h + 1] = jnp.where(lane >= A_HEAD_DIM, qh, zero)

    def block(j, diagonal):
        start = pl.multiple_of(j * t, t)

        def scores(i):
            h = i // 2
            s = jnp.dot(qz_ref[i], kt_ref[j, h * HEAD_W:(h + 1) * HEAD_W, :], preferred_element_type=F32)
            if diagonal:
                row = lax.broadcasted_iota(jnp.int32, s.shape, 0)
                col = lax.broadcasted_iota(jnp.int32, s.shape, 1)
                s = jnp.where(col <= row, s, -jnp.inf)
            return s

        s_next = scores(0)
        for i in range(n_maps):
            s = s_next
            if i + 1 < n_maps:
                s_next = scores(i + 1)
            h = i // 2
            vc = v_ref[pl.ds(start, t), h * HEAD_W:(h + 1) * HEAD_W]
            cols = [s[:, cc * HEAD_W:(cc + 1) * HEAD_W] for cc in range(t // HEAD_W)]
            m_new = jnp.broadcast_to(jnp.max(functools.reduce(jnp.maximum, cols), axis=1, keepdims=True),
                                     (t, HEAD_W))
            if not diagonal:
                m_old = m_ref[i]
                m_new = jnp.maximum(m_old, m_new)
                alpha = jnp.exp2(m_old - m_new)
            ps = [jnp.exp2((cl - m_new).astype(BF16)) for cl in cols]
            l_new = functools.reduce(jnp.add, ps).astype(F32)
            pv = jnp.dot(jnp.concatenate(ps, axis=1), vc, preferred_element_type=F32)
            if not diagonal:
                l_new = alpha * l_ref[i] + l_new
                pv = alpha * acc_ref[i] + pv
            l_ref[i] = l_new
            acc_ref[i] = pv
            m_ref[i] = m_new

    def body(j, carry):
        block(j, False)
        return carry

    block(qi, True)
    lax.fori_loop(0, qi, body, 0)
    lam = _lambda(lam_ref, lam_init)
    g = g_ref[...]
    for h in range(A_HEADS):
        l0 = jnp.sum(l_ref[2 * h], axis=1, keepdims=True)
        l1 = jnp.sum(l_ref[2 * h + 1], axis=1, keepdims=True)
        o = acc_ref[2 * h] / l0 - lam * (acc_ref[2 * h + 1] / l1)
        o_ref[:, h * HEAD_W:(h + 1) * HEAD_W] = (_rms(o, g) * (1.0 - lam_init)).astype(BF16)


def _attn_sample_kernel(pt_ref, lam_ref, g_ref, q_ref, kn_ref, vn_ref, *refs, lam_init, n_pages):
    del pt_ref
    k_refs = refs[:n_pages]
    v_refs = refs[n_pages:2 * n_pages]
    o_ref = refs[2 * n_pages]
    n_maps = 2 * A_HEADS
    q = q_ref[...]
    row = lax.broadcasted_iota(jnp.int32, (n_maps, A_WIDTH), 0)
    lane = lax.broadcasted_iota(jnp.int32, (n_maps, A_WIDTH), 1)
    owner = (row % A_HEADS) * 2 + row // A_HEADS
    qbd = jnp.where(lane // A_HEAD_DIM == owner, jnp.broadcast_to(q, (n_maps, A_WIDTH)), 0.0)
    qbd_b = qbd.astype(BF16)
    kt = jnp.concatenate([k_refs[j][...].astype(BF16) for j in range(n_pages)], axis=1)
    s = jnp.dot(qbd_b, kt, preferred_element_type=F32)
    s_new = jnp.sum(qbd * kn_ref[...], axis=1, keepdims=True)
    m = jnp.maximum(jnp.max(s, axis=1, keepdims=True), s_new)
    e = jnp.exp(s - m)
    e_new = jnp.exp(s_new - m)
    l = jnp.sum(e, axis=1, keepdims=True) + e_new
    p = e / l
    p_new = e_new / l
    lam = _lambda(lam_ref, lam_init)
    a = (p[0:A_HEADS] - lam * p[A_HEADS:n_maps]).astype(BF16)
    a_new = p_new[0:A_HEADS] - lam * p_new[A_HEADS:n_maps]
    outs = []
    for h in range(A_HEADS):
        vh = jnp.concatenate(
            [v_refs[j][pl.ds(h, PAGE_SIZE, stride=A_HEADS), :].astype(BF16) for j in range(n_pages)], axis=0)
        outs.append(jnp.dot(a[h:h + 1, :], vh, preferred_element_type=F32))
    o = jnp.concatenate(outs, axis=0) + a_new * vn_ref[...]
    o_ref[...] = _rms(o, g_ref[...]) * (1.0 - lam_init)


def _post_kernel(x_ref, a_ref, bc_ref, pe_ref, wo_ref, w1_ref, w3_ref, w2_ref,
                 wpg_ref, wple_ref, gf_ref, o_ref, *, final):
    x = x_ref[...]
    x = x + _mm(a_ref[...], wo_ref[0:A_WIDTH, :]) + _mm(bc_ref[...], wo_ref[A_WIDTH:, :])
    x = _ffn(x, w1_ref[...], w3_ref[...], w2_ref[...])
    gate = jax.nn.sigmoid(_normed_mm(x, wpg_ref[...]))
    x = x + gate * _mm(pe_ref[...], wple_ref[...])
    if final:
        x = _rms(x, gf_ref[...])
    o_ref[...] = x


def _const_spec(shape):
    nd = len(shape)
    return pl.BlockSpec(shape, lambda *_: (0,) * nd, pipeline_mode=pl.Buffered(1))


def _layer_spec(stacked, layer):
    rest = stacked.shape[1:]
    return pl.BlockSpec((None,) + rest, lambda *_: (layer,) + (0,) * len(rest),
                        pipeline_mode=pl.Buffered(1))


def _row_spec(tm, width):
    return pl.BlockSpec((tm, width), lambda i: (i, 0))


def _params(n_axes):
    return pltpu.CompilerParams(dimension_semantics=("arbitrary",) * n_axes,
                                vmem_limit_bytes=VMEM_LIMIT_BYTES)


def _pre_prompt(x, cos, sin, w, layer, kv_out, *, depth, seq, tm, tk):
    t, d = x.shape
    tiles_per_seq = seq // tm
    per_chunk = tk // tm
    n_tiles = t // tm
    cur = lambda i: jnp.minimum(i, n_tiles - 1)
    lag = lambda i: jnp.maximum(i - 1, 0)
    consts = [w[k] for k in ("w1_ffn1", "w3_ffn1", "w2_ffn1", "w_in",
                             "g_sgu", "wcat", "bfull", "wpool_t", "s_pool")]
    args = [x, cos, sin] + consts
    in_specs = [pl.BlockSpec((tm, d), lambda i: (cur(i), 0)),
                pl.BlockSpec((tm, HEAD_W), lambda i: (lag(i) % tiles_per_seq, 0)),
                pl.BlockSpec((tm, HEAD_W), lambda i: (lag(i) % tiles_per_seq, 0))]
    in_specs += [_layer_spec(c, layer) for c in consts]
    aliases = {}
    if kv_out is not None:
        aliases = {len(args): 4, len(args) + 1: 5}
        args += list(kv_out)
        in_specs += [pl.BlockSpec(memory_space=pl.ANY)] * 2
    out_shape = (
        jax.ShapeDtypeStruct((t, d), F32),
        jax.ShapeDtypeStruct((t, A_WIDTH), BF16),
        jax.ShapeDtypeStruct((t // seq, seq // tk, A_WIDTH, tk), BF16),
        jax.ShapeDtypeStruct((t, A_WIDTH), BF16),
        jax.ShapeDtypeStruct((depth, t // seq, A_WIDTH, seq), F32),
        jax.ShapeDtypeStruct((depth * t * A_HEADS, HEAD_W), F32),
        jax.ShapeDtypeStruct((t, B_WIDTH + C_WIDTH), BF16),
        jax.ShapeDtypeStruct((t // seq, HALO, C_WIDTH), F32),
    )
    lag_rows = lambda width: pl.BlockSpec((tm, width), lambda i: (lag(i), 0))
    out_specs = (
        pl.BlockSpec((tm, d), lambda i: (cur(i), 0)),
        lag_rows(A_WIDTH),
        pl.BlockSpec((None, None, A_WIDTH, tm),
                     lambda i: (lag(i) // tiles_per_seq, (lag(i) % tiles_per_seq) // per_chunk, 0,
                                lag(i) % per_chunk)),
        lag_rows(A_WIDTH),
        pl.BlockSpec((None, None, A_WIDTH, tm),
                     lambda i: (layer, lag(i) // tiles_per_seq, 0, lag(i) % tiles_per_seq)),
        pl.BlockSpec((tm * A_HEADS, HEAD_W), lambda i: (layer * n_tiles + lag(i), 0)),
        lag_rows(B_WIDTH + C_WIDTH),
        pl.BlockSpec((None, HALO, C_WIDTH), lambda i: (lag(i) // tiles_per_seq, 0, 0)),
    )
    return pl.pallas_call(
        functools.partial(_pre_prompt_kernel, tm=tm, tiles_per_seq=tiles_per_seq, n_aliased=len(aliases)),
        grid=(n_tiles + 1,),
        in_specs=in_specs,
        out_specs=out_specs,
        out_shape=out_shape,
        input_output_aliases=aliases,
        scratch_shapes=[pltpu.VMEM((tm, w["w_in"].shape[2]), F32),
                        pltpu.VMEM((tm + HALO, C_WIDTH), F32)],
        compiler_params=_params(1),
        name="pre_prompt",
    )(*args)


def _pre_sample(x, cos, sin, state, w, layer):
    m, d = x.shape
    consts = [w[k] for k in ("w1_ffn1", "w3_ffn1", "w2_ffn1", "w_in", "g_sgu", "sgu0")]
    consts += [state, w["wpool_t"], w["s_pool"]]
    args = [x, cos, sin] + consts
    out_shape = (
        jax.ShapeDtypeStruct((m, d), F32),
        jax.ShapeDtypeStruct((m, A_WIDTH), F32),
        jax.ShapeDtypeStruct((m, A_WIDTH), F32),
        jax.ShapeDtypeStruct((m, A_WIDTH), F32),
        jax.ShapeDtypeStruct((m, B_WIDTH + C_WIDTH), BF16),
        jax.ShapeDtypeStruct((m, C_WIDTH), F32),
        jax.ShapeDtypeStruct((m, B_WIDTH), F32),
    )
    return pl.pallas_call(
        _pre_sample_kernel,
        grid=(1,),
        in_specs=[_const_spec(a.shape) for a in args[:3]] + [_layer_spec(c, layer) for c in consts],
        out_specs=tuple(pl.BlockSpec(o.shape, lambda i: (0, 0)) for o in out_shape),
        out_shape=out_shape,
        compiler_params=_params(1),
        name="pre_sample",
    )(*args)


def _attn_prompt(qb, ktb, vb, lam_qk, g_subln, *, layer, batch, seq):
    n_chunks, t = ktb.shape[1], ktb.shape[3]
    n_maps = 2 * A_HEADS
    return pl.pallas_call(
        functools.partial(_attn_prompt_kernel, lam_init=_lam_init(layer), t=t),
        grid=(batch, n_chunks),
        in_specs=[
            pl.BlockSpec((None,) + lam_qk.shape[1:], lambda b, i: (layer, 0, 0)),
            pl.BlockSpec((None,) + g_subln.shape[1:], lambda b, i: (layer, 0, 0)),
            pl.BlockSpec((t, A_WIDTH), lambda b, i: (b * n_chunks + i, 0)),
            pl.BlockSpec((None, n_chunks, A_WIDTH, t), lambda b, i: (b, 0, 0, 0),
                         pipeline_mode=pl.Buffered(1)),
            pl.BlockSpec((seq, A_WIDTH), lambda b, i: (b, 0), pipeline_mode=pl.Buffered(1)),
        ],
        out_specs=pl.BlockSpec((t, A_WIDTH), lambda b, i: (b * n_chunks + i, 0)),
        out_shape=jax.ShapeDtypeStruct(qb.shape, BF16),
        scratch_shapes=[pltpu.VMEM((n_maps, t, HEAD_W), BF16),
                        pltpu.VMEM((n_maps, t, HEAD_W), F32),
                        pltpu.VMEM((n_maps, t, HEAD_W), F32),
                        pltpu.VMEM((n_maps, t, HEAD_W), F32)],
        compiler_params=_params(2),
        name="attn_prompt",
    )(lam_qk, g_subln, qb, ktb, vb)


def _attn_sample(q, k_new, v_new, cache_kt, cache_v, page_table, lam_qk, g_subln, *, layer):
    nb, n_pages = page_table.shape

    def page_spec(j, shape):
        return pl.BlockSpec((None, None) + shape, lambda b, pt: (layer, pt[b, j], 0, 0))

    def row_spec(shape):
        return pl.BlockSpec((None,) + shape, lambda b, pt: (b, 0, 0))

    in_specs = [pl.BlockSpec((None,) + lam_qk.shape[1:], lambda b, pt: (layer, 0, 0)),
                pl.BlockSpec((None,) + g_subln.shape[1:], lambda b, pt: (layer, 0, 0)),
                row_spec((1, A_WIDTH)), row_spec((1, A_WIDTH)), row_spec((A_HEADS, HEAD_W))]
    in_specs += [page_spec(j, (A_WIDTH, PAGE_SIZE)) for j in range(n_pages)]
    in_specs += [page_spec(j, (PAGE_SIZE * A_HEADS, HEAD_W)) for j in range(n_pages)]
    grid_spec = pltpu.PrefetchScalarGridSpec(
        num_scalar_prefetch=1,
        grid=(nb,),
        in_specs=in_specs,
        out_specs=row_spec((A_HEADS, HEAD_W)),
    )
    return pl.pallas_call(
        functools.partial(_attn_sample_kernel, lam_init=_lam_init(layer), n_pages=n_pages),
        grid_spec=grid_spec,
        out_shape=jax.ShapeDtypeStruct((nb, A_HEADS, HEAD_W), F32),
        compiler_params=_params(1),
        name="attn_sample",
    )(page_table, lam_qk, g_subln, q, k_new, v_new,
      *([cache_kt] * n_pages), *([cache_v] * n_pages))


def _post(x, a, bc, pe, w, layer, g_final, *, tm, final):
    t, d = x.shape
    consts = [w[k] for k in ("w_out", "w1_ffn2", "w3_ffn2", "w2_ffn2", "w_pg", "w_ple")]
    in_specs = [_row_spec(tm, d), _row_spec(tm, a.shape[1]), _row_spec(tm, bc.shape[1]),
                pl.BlockSpec((None, tm, pe.shape[2]), lambda i: (layer, i, 0))]
    in_specs += [_layer_spec(c, layer) for c in consts] + [_const_spec(g_final.shape)]
    return pl.pallas_call(
        functools.partial(_post_kernel, final=final),
        grid=(t // tm,),
        in_specs=in_specs,
        out_specs=_row_spec(tm, d),
        out_shape=jax.ShapeDtypeStruct((t, d), F32),
        compiler_params=_params(1),
        name="post",
    )(x, a, bc, pe, *consts, g_final)


def _rope_tables(pos):
    half = A_HEAD_DIM // 2
    inv = ROPE_THETA ** (-jnp.arange(half, dtype=F32) / half)
    ang = pos.astype(F32)[:, None] * inv[None, :]
    reps = HEAD_W // half
    cos = jnp.tile(jnp.cos(ang), (1, reps))
    sin = jnp.tile(jnp.sin(ang), (1, reps))
    sign = jnp.where((jnp.arange(HEAD_W) % A_HEAD_DIM) < half, -1.0, 1.0).astype(F32)
    return cos, sin * sign[None, :]


def _prepare_weights(p):
    depth = p["g_mix"].shape[0]
    row = lambda a: a.reshape(depth, 1, -1)
    bf = lambda a: a.astype(BF16)
    gained = lambda g, a: (g[:, :, None] * a).astype(BF16)
    w_sgu = p["w_sgu"]
    b_sgu = p["b_sgu"]
    per_head = B_WIDTH // B_HEADS
    return {
        "w1_ffn1": gained(p["g_ffn1"], p["w1_ffn1"]), "w3_ffn1": gained(p["g_ffn1"], p["w3_ffn1"]),
        "w2_ffn1": bf(p["w2_ffn1"]),
        "w_in": gained(p["g_mix"], p["w_in"]),
        "g_sgu": row(p["g_sgu"]),
        "wcat": jnp.transpose(w_sgu, (0, 2, 1, 3)).reshape(depth, CHUNK, B_HEADS * CHUNK),
        "bfull": jnp.repeat(jnp.transpose(b_sgu, (0, 2, 1)), per_head, axis=2),
        "sgu0": jnp.stack([jnp.repeat(w_sgu[:, :, 0, 0], per_head, axis=1),
                           jnp.repeat(b_sgu[:, :, 0], per_head, axis=1)], axis=1),
        "wpool_t": jnp.tile(p["w_pool"].reshape(depth, C_WIDTH, C_GROUP_DIM), (1, 1, len(POOL_WINDOWS))),
        "s_pool": row(p["s_pool"]),
        "w_out": bf(p["w_out"]),
        "w1_ffn2": gained(p["g_ffn2"], p["w1_ffn2"]), "w3_ffn2": gained(p["g_ffn2"], p["w3_ffn2"]),
        "w2_ffn2": bf(p["w2_ffn2"]),
        "w_pg": gained(p["g_ple"], p["w_pg"]), "w_ple": bf(p["w_ple"]),
        "lam_qk": p["lam_qk"], "g_subln": row(p["g_subln"]),
    }


def kernel(x_prompt, x_sample, cache_k, cache_v, state_pool, page_table, p_prompt, p_sample, g_ffn1, w1_ffn1, w3_ffn1, w2_ffn1, g_mix, w_in, lam_qk, g_subln, g_sgu, w_sgu, b_sgu, w_pool, s_pool, w_out, g_ffn2, w1_ffn2, w3_ffn2, w2_ffn2, g_ple, w_pg, w_ple, g_final):
    params = dict(g_ffn1=g_ffn1, w1_ffn1=w1_ffn1, w3_ffn1=w3_ffn1, w2_ffn1=w2_ffn1, g_mix=g_mix,
                  w_in=w_in, lam_qk=lam_qk, g_subln=g_subln, g_sgu=g_sgu, w_sgu=w_sgu, b_sgu=b_sgu,
                  w_pool=w_pool, s_pool=s_pool, w_out=w_out, g_ffn2=g_ffn2, w1_ffn2=w1_ffn2,
                  w3_ffn2=w3_ffn2, w2_ffn2=w2_ffn2, g_ple=g_ple, w_pg=w_pg, w_ple=w_ple)
    batch, seq, d = x_prompt.shape
    nb, dec_seq, _ = x_sample.shape
    assert dec_seq == 1
    depth = cache_k.shape[0]
    n_pool = cache_k.shape[1]
    past_len = page_table.shape[1] * PAGE_SIZE
    tm = 256
    tq = 512

    cos_p, sin_p = _rope_tables(jnp.arange(seq, dtype=jnp.int32))
    cos_s, sin_s = _rope_tables(jnp.full((nb,), past_len, dtype=jnp.int32))
    ckt = jnp.transpose(cache_k, (0, 1, 3, 4, 5, 2)).reshape(depth, n_pool, A_WIDTH, PAGE_SIZE)
    cv = cache_v.reshape(depth, n_pool, PAGE_SIZE * A_HEADS, HEAD_W)
    g_fin = g_final.reshape(1, -1)
    w = _prepare_weights(params)
    state_t = jnp.transpose(state_pool, (0, 2, 1, 3))
    pe_p = p_prompt.reshape(depth, batch * seq, -1)
    pe_s = p_sample.reshape(depth, nb, -1)

    xp = x_prompt.reshape(batch * seq, d)
    xs = x_sample.reshape(nb, d)
    kv_out = None
    ks_l, vs_l, pp_l, ps_l, sv_l = [], [], [], [], []
    for l in range(depth):
        final = l == depth - 1

        xmid, qb, ktb, vb, ktf, vf, bc, ctail = _pre_prompt(xp, cos_p, sin_p, w, l, kv_out,
                                                            depth=depth, seq=seq, tm=tm, tk=tq)
        kv_out = (ktf, vf)
        a = _attn_prompt(qb, ktb, vb, w["lam_qk"], w["g_subln"], layer=l, batch=batch, seq=seq)
        xp = _post(xmid, a, bc, pe_p, w, l, g_fin, tm=tm, final=final)
        pp_l.append(ctail[:, HALO - POOL_BUF:, :])

        xmid, q, k, v, bc, c, sv = _pre_sample(xs, cos_s, sin_s, state_t, w, l)
        a = _attn_sample(q.reshape(nb, 1, A_WIDTH), k.reshape(nb, 1, A_WIDTH), v.reshape(nb, A_HEADS, HEAD_W),
                         ckt, cv, page_table, w["lam_qk"], w["g_subln"], layer=l)
        xs = _post(xmid, a.reshape(nb, A_WIDTH), bc, pe_s, w, l, g_fin, tm=nb, final=final)
        ks_l.append(k.reshape(nb, 1, A_HEADS, 2, A_HEAD_DIM))
        vs_l.append(v.reshape(nb, 1, A_HEADS, HEAD_W))
        ps_l.append(jnp.concatenate([state_pool[l][:, 1:, :], c[:, None, :]], axis=1))
        sv_l.append(sv.reshape(nb, 1, B_WIDTH))

    ktf, vf = kv_out
    k_prompt = jnp.transpose(ktf.reshape(depth, batch, A_HEADS, 2, A_HEAD_DIM, seq), (0, 1, 5, 2, 3, 4))
    v_prompt = vf.reshape(depth, batch, seq, A_HEADS, HEAD_W)
    return (xp.reshape(batch, seq, d), xs.reshape(nb, 1, d),
            k_prompt, v_prompt, jnp.stack(ks_l), jnp.stack(vs_l),
            jnp.stack(pp_l), jnp.stack(ps_l), jnp.stack(sv_l))
```

```python
import functools
import math

import jax
import jax.numpy as jnp
from jax import lax
from jax.experimental import pallas as pl
from jax.experimental.pallas import tpu as pltpu

F32 = jnp.float32
BF16 = jnp.bfloat16

EPS = 1e-6
ROPE_THETA = 10000.0
A_HEADS = 4
A_HEAD_DIM = 64
HEAD_W = 2 * A_HEAD_DIM
A_WIDTH = A_HEADS * HEAD_W
B_WIDTH = 256
B_HEADS = 4
CHUNK = 128
C_WIDTH = 256
POOL_WINDOWS = (2, 4, 8, 16)
C_GROUP_DIM = C_WIDTH // len(POOL_WINDOWS)
HALO = 16
POOL_BUF = 15
ATTN_SCALE = A_HEAD_DIM ** -0.5
LOG2_E = math.log2(math.e)
PAGE_SIZE = 128

VMEM_LIMIT_BYTES = 56 * 1024 * 1024


def _lam_init(layer):
    return 0.8 - 0.6 * math.exp(-0.3 * layer)


def _rms(x, g):
    return x * lax.rsqrt(jnp.mean(x * x, axis=-1, keepdims=True) + EPS) * g


def _mm(a, w):
    return jnp.dot(a.astype(BF16), w, preferred_element_type=F32)


def _normed_mm(x, wg):
    return _mm(x, wg) * lax.rsqrt(jnp.mean(x * x, axis=-1, keepdims=True) + EPS)


def _ffn(x, w1g, w3g, w2):
    xb = x.astype(BF16)
    r = lax.rsqrt(jnp.mean(x * x, axis=-1, keepdims=True) + EPS)
    a = jnp.dot(xb, w1g, preferred_element_type=F32) * r
    b = jnp.dot(xb, w3g, preferred_element_type=F32) * r
    return x + 0.5 * _mm(a * jax.nn.sigmoid(a) * b, w2)


def _rope(z, cos, sin_signed):
    m = z.shape[0]
    lane = lax.broadcasted_iota(jnp.int32, (m, HEAD_W), 1)
    first_half = (lane % A_HEAD_DIM) < (A_HEAD_DIM // 2)
    outs = []
    for j in range(z.shape[1] // HEAD_W):
        blk = z[:, j * HEAD_W:(j + 1) * HEAD_W]
        from_right = pltpu.roll(blk, HEAD_W - A_HEAD_DIM // 2, axis=1)
        from_left = pltpu.roll(blk, A_HEAD_DIM // 2, axis=1)
        partner = jnp.where(first_half, from_right, from_left)
        outs.append(blk * cos + partner * sin_signed)
    return jnp.concatenate(outs, axis=1)


def _pool_window_lanes(shape):
    lane = lax.broadcasted_iota(jnp.int32, shape, len(shape) - 1)
    w = jnp.full(shape, POOL_WINDOWS[-1], jnp.int32)
    for gi in range(len(POOL_WINDOWS) - 2, -1, -1):
        w = jnp.where(lane < (gi + 1) * C_GROUP_DIM, POOL_WINDOWS[gi], w)
    return w


def _pool_weight(wpool_tiled):
    r = lax.broadcasted_iota(jnp.int32, wpool_tiled.shape, 0) // C_GROUP_DIM
    c = lax.broadcasted_iota(jnp.int32, wpool_tiled.shape, 1) // C_GROUP_DIM
    return jnp.where(r == c, wpool_tiled, 0.0).astype(BF16)


def _split_in(z):
    o = 0
    q = z[:, o:o + A_WIDTH]; o += A_WIDTH
    k = z[:, o:o + A_WIDTH]; o += A_WIDTH
    v = z[:, o:o + A_WIDTH]; o += A_WIDTH
    hu = z[:, o:o + B_WIDTH]; o += B_WIDTH
    hv = z[:, o:o + B_WIDTH]; o += B_WIDTH
    c = z[:, o:o + C_WIDTH]
    return q, k, v, hu, hv, c


def _pre_prompt_kernel(x_ref, cos_ref, sin_ref, w1_ref, w3_ref, w2_ref, win_ref,
                       gs_ref, wcat_ref, bfull_ref, wpool_ref, spool_ref, *rest,
                       tm, tiles_per_seq, n_aliased, layer):
    (xmid_ref, qb_ref, ktb_ref, vb_ref, ktf_ref, vf_ref, bc_ref, ctail_ref, z_ref, cext_ref) = rest[n_aliased:]
    if not n_aliased:
        for other in range(ktf_ref.shape[0]):
            if other != layer:
                ktf_ref[other] = jnp.zeros(ktf_ref.shape[1:], F32)
                vf_ref[other] = jnp.zeros(vf_ref.shape[1:], F32)
        ktf_ref = ktf_ref.at[layer]
        vf_ref = vf_ref.at[layer]
    i = pl.program_id(0)

    @pl.when(i == 0)
    def _():
        z_ref[...] = jnp.zeros(z_ref.shape, F32)
        cext_ref[...] = jnp.zeros(cext_ref.shape, F32)

    tile = jnp.maximum(i - 1, 0) % tiles_per_seq
    q, k, v, hu, hv, c = _split_in(z_ref[...])
    cos = cos_ref[...]
    sin = sin_ref[...]
    q = _rope(q, cos, sin)
    k = _rope(k, cos, sin)
    qb_ref[...] = (q * (ATTN_SCALE * LOG2_E)).astype(BF16)
    kt = k.T
    ktf_ref[...] = kt
    ktb_ref[...] = kt.astype(BF16)
    vb_ref[...] = v.astype(BF16)
    for h in range(A_HEADS):
        vf_ref[pl.ds(h, tm, stride=A_HEADS), :] = v[:, h * HEAD_W:(h + 1) * HEAD_W]
    u = jax.nn.gelu(hu)
    vn = _rms(jax.nn.gelu(hv), gs_ref[...])

    cext_ref[0:HALO, :] = jnp.where(tile == 0, 0.0, cext_ref[tm:tm + HALO, :])
    cext_ref[HALO:HALO + tm, :] = c
    ctail_ref[...] = c[tm - HALO:tm, :]
    acc = c
    sums = {}
    for j in range(1, POOL_WINDOWS[-1]):
        acc = acc + cext_ref[HALO - j:HALO - j + tm, :]
        if (j + 1) in POOL_WINDOWS:
            sums[j + 1] = acc
    wl = _pool_window_lanes((tm, C_WIDTH))
    s = sums[POOL_WINDOWS[-1]]
    for w in POOL_WINDOWS[-2::-1]:
        s = jnp.where(wl == w, sums[w], s)
    pos = tile * tm + lax.broadcasted_iota(jnp.int32, (tm, C_WIDTH), 0)
    cnt = jnp.minimum(pos + 1, wl).astype(F32)
    d = s / cnt - c

    xm = _ffn(x_ref[...], w1_ref[...], w3_ref[...], w2_ref[...])
    xmid_ref[...] = xm
    z_ref[...] = _normed_mm(xm, win_ref[...])

    wr = lax.broadcasted_iota(jnp.int32, (CHUNK, B_HEADS * CHUNK), 0)
    wc = lax.broadcasted_iota(jnp.int32, (CHUNK, B_HEADS * CHUNK), 1)
    wm = jnp.where((wc % CHUNK) <= wr, wcat_ref[...], 0.0).astype(BF16)
    hr = lax.broadcasted_iota(jnp.int32, (B_HEADS * CHUNK, B_WIDTH), 0) // CHUNK
    hc = lax.broadcasted_iota(jnp.int32, (B_HEADS * CHUNK, B_WIDTH), 1) // (B_WIDTH // B_HEADS)
    head_mask = hr == hc
    bfull = bfull_ref[...]
    for ci in range(tm // CHUNK):
        rows = slice(ci * CHUNK, (ci + 1) * CHUNK)
        vc = vn[rows].astype(BF16)
        vbd = jnp.where(head_mask, jnp.concatenate([vc] * B_HEADS, axis=0), jnp.zeros((), BF16))
        zc = jnp.dot(wm, vbd, preferred_element_type=F32) + bfull
        bc_ref[rows, 0:B_WIDTH] = (u[rows] * zc).astype(BF16)
    co = _mm(d, _pool_weight(wpool_ref[...])) * spool_ref[...]
    bc_ref[:, B_WIDTH:B_WIDTH + C_WIDTH] = co.astype(BF16)


def _pre_sample_kernel(x_ref, cos_ref, sin_ref, w1_ref, w3_ref, w2_ref, win_ref,
                       gs_ref, sgu0_ref, state_ref, wpool_ref, spool_ref,
                       xmid_ref, q_ref, k_ref, v_ref, bc_ref, c_ref, sv_ref):
    xm = _ffn(x_ref[...], w1_ref[...], w3_ref[...], w2_ref[...])
    xmid_ref[...] = xm
    z = _normed_mm(xm, win_ref[...])
    q, k, v, hu, hv, c = _split_in(z)
    cos = cos_ref[...]
    sin = sin_ref[...]
    q_ref[...] = _rope(q, cos, sin) * ATTN_SCALE
    k_ref[...] = _rope(k, cos, sin)
    v_ref[...] = v
    c_ref[...] = c

    u = jax.nn.gelu(hu)
    vn = _rms(jax.nn.gelu(hv), gs_ref[...])
    sv_ref[...] = vn
    sgu0 = sgu0_ref[...]
    bc_ref[:, 0:B_WIDTH] = (u * (vn * sgu0[0:1, :] + sgu0[1:2, :])).astype(BF16)

    state = state_ref[...]
    wl3 = _pool_window_lanes(state.shape)
    r3 = lax.broadcasted_iota(jnp.int32, state.shape, 0)
    s = jnp.sum(jnp.where(r3 >= (POOL_BUF + 1) - wl3, state, 0.0), axis=0) + c
    d = s / _pool_window_lanes(c.shape).astype(F32) - c
    co = _mm(d, _pool_weight(wpool_ref[...])) * spool_ref[...]
    bc_ref[:, B_WIDTH:B_WIDTH + C_WIDTH] = co.astype(BF16)


def _lambda(lam_ref, lam_init):
    lq = lam_ref[...]
    s01 = jnp.sum(lq[0:1, :] * lq[1:2, :], axis=1, keepdims=True)
    s23 = jnp.sum(lq[2:3, :] * lq[3:4, :], axis=1, keepdims=True)
    return jnp.exp(s01) - jnp.exp(s23) + lam_init


def _attn_kernel(pt_ref, lam_ref, g_ref, q_ref, kt_ref, v_ref, qs_ref, kn_ref, vn_ref, ck_hbm, cv_hbm,
                 o_ref, os_ref, qz_ref, m_ref, l_ref, acc_ref, kbuf, vbuf, sem,
                 *, lam_init, t, layer, n_pages, rows_per_step):
    qi = pl.program_id(1)
    step = pl.program_id(0) * pl.num_programs(1) + qi
    last_step = pl.num_programs(0) * pl.num_programs(1) - 1
    n_maps = 2 * A_HEADS

    def page_copies(r, slot):
        copies = []
        for j in range(n_pages):
            page = pt_ref[r, j]
            copies.append(pltpu.make_async_copy(ck_hbm.at[layer, page], kbuf.at[slot, j], sem.at[slot]))
            copies.append(pltpu.make_async_copy(cv_hbm.at[layer, page], vbuf.at[slot, j], sem.at[slot]))
        return copies

    def start_row(r, slot):
        for cp in page_copies(r, slot):
            cp.start()

    def sample_row(r, slot):
        for cp in page_copies(r, slot):
            cp.wait()
        k_pages = [kbuf[slot, j] for j in range(n_pages)]
        v_rows = lambda h: [vbuf[slot, j, pl.ds(h, PAGE_SIZE, stride=A_HEADS), :] for j in range(n_pages)]
        os_ref[r] = _sample_attend(qs_ref[r], kn_ref[r], vn_ref[r], k_pages, v_rows,
                                   _lambda(lam_ref, lam_init), g_ref[...], lam_init)

    first_row = step * rows_per_step

    @pl.when(step == 0)
    def _():
        start_row(0, 0)
        start_row(1, 1)

    def sample_rows(lo, hi, refill):
        for i in range(lo, hi):
            sample_row(first_row + i, i % 2)
            if refill is None:
                start_row(first_row + i + 2, i % 2)
            else:
                @pl.when(refill)
                def _():
                    start_row(first_row + i + 2, i % 2)
    lane = lax.broadcasted_iota(jnp.int32, (t, HEAD_W), 1)
    zero = jnp.zeros((), BF16)
    for h in range(A_HEADS):
        qh = q_ref[:, h * HEAD_W:(h + 1) * HEAD_W]
        qz_ref[2 * h] = jnp.where(lane < A_HEAD_DIM, qh, zero)
        qz_ref[2 * h + 1] = jnp.where(lane >= A_HEAD_DIM, qh, zero)

    def block(j, diagonal):
        start = pl.multiple_of(j * t, t)

        def scores(i):
            h = i // 2
            s = jnp.dot(qz_ref[i], kt_ref[j, h * HEAD_W:(h + 1) * HEAD_W, :], preferred_element_type=F32)
            if diagonal:
                row = lax.broadcasted_iota(jnp.int32, s.shape, 0)
                col = lax.broadcasted_iota(jnp.int32, s.shape, 1)
                s = jnp.where(col <= row, s, -jnp.inf)
            return s

        s_next = scores(0)
        for i in range(n_maps):
            s = s_next
            if i + 1 < n_maps:
                s_next = scores(i + 1)
            h = i // 2
            vc = v_ref[pl.ds(start, t), h * HEAD_W:(h + 1) * HEAD_W]
            cols = [s[:, cc * HEAD_W:(cc + 1) * HEAD_W] for cc in range(t // HEAD_W)]
            m_new = jnp.broadcast_to(jnp.max(functools.reduce(jnp.maximum, cols), axis=1, keepdims=True),
                                     (t, HEAD_W))
            if not diagonal:
                m_old = m_ref[i]
                m_new = jnp.maximum(m_old, m_new)
                alpha = jnp.exp2(m_old - m_new)
            ps = [jnp.exp2((cl - m_new).astype(BF16)) for cl in cols]
            l_new = functools.reduce(jnp.add, ps).astype(F32)
            pv = jnp.dot(jnp.concatenate(ps, axis=1), vc, preferred_element_type=F32)
            if not diagonal:
                l_new = alpha * l_ref[i] + l_new
                pv = alpha * acc_ref[i] + pv
            l_ref[i] = l_new
            acc_ref[i] = pv
            m_ref[i] = m_new

    def body(j, carry):
        block(j, False)
        return carry

    half = rows_per_step // 2
    sample_rows(0, half, None)
    block(qi, True)
    sample_rows(half, rows_per_step, step < last_step)
    lax.fori_loop(0, qi, body, 0)
    lam = _lambda(lam_ref, lam_init)
    g = g_ref[...]
    for h in range(A_HEADS):
        l0 = jnp.sum(l_ref[2 * h], axis=1, keepdims=True)
        l1 = jnp.sum(l_ref[2 * h + 1], axis=1, keepdims=True)
        o = acc_ref[2 * h] / l0 - lam * (acc_ref[2 * h + 1] / l1)
        o_ref[:, h * HEAD_W:(h + 1) * HEAD_W] = (_rms(o, g) * (1.0 - lam_init)).astype(BF16)


def _sample_attend(q, k_new, v_new, k_pages, v_rows, lam, g, lam_init):
    n_maps = 2 * A_HEADS
    row = lax.broadcasted_iota(jnp.int32, (n_maps, A_WIDTH), 0)
    lane = lax.broadcasted_iota(jnp.int32, (n_maps, A_WIDTH), 1)
    owner = (row % A_HEADS) * 2 + row // A_HEADS
    qbd = jnp.where(lane // A_HEAD_DIM == owner, jnp.broadcast_to(q, (n_maps, A_WIDTH)), 0.0)
    qbd_b = qbd.astype(BF16)
    kt = jnp.concatenate([kp.astype(BF16) for kp in k_pages], axis=1)
    s = jnp.dot(qbd_b, kt, preferred_element_type=F32)
    s_new = jnp.sum(qbd * k_new, axis=1, keepdims=True)
    m = jnp.maximum(jnp.max(s, axis=1, keepdims=True), s_new)
    e = jnp.exp(s - m)
    e_new = jnp.exp(s_new - m)
    l = jnp.sum(e, axis=1, keepdims=True) + e_new
    p = e / l
    p_new = e_new / l
    a = (p[0:A_HEADS] - lam * p[A_HEADS:n_maps]).astype(BF16)
    a_new = p_new[0:A_HEADS] - lam * p_new[A_HEADS:n_maps]
    outs = []
    for h in range(A_HEADS):
        vh = jnp.concatenate([vp.astype(BF16) for vp in v_rows(h)], axis=0)
        outs.append(jnp.dot(a[h:h + 1, :], vh, preferred_element_type=F32))
    o = jnp.concatenate(outs, axis=0) + a_new * v_new
    return _rms(o, g) * (1.0 - lam_init)


def _post_kernel(x_ref, a_ref, bc_ref, pe_ref, wo_ref, w1_ref, w3_ref, w2_ref,
                 wpg_ref, wple_ref, gf_ref, o_ref, *, final):
    x = x_ref[...]
    x = x + _mm(a_ref[...], wo_ref[0:A_WIDTH, :]) + _mm(bc_ref[...], wo_ref[A_WIDTH:, :])
    x = _ffn(x, w1_ref[...], w3_ref[...], w2_ref[...])
    gate = jax.nn.sigmoid(_normed_mm(x, wpg_ref[...]))
    x = x + gate * _mm(pe_ref[...], wple_ref[...])
    if final:
        x = _rms(x, gf_ref[...])
    o_ref[...] = x


def _const_spec(shape):
    nd = len(shape)
    return pl.BlockSpec(shape, lambda *_: (0,) * nd, pipeline_mode=pl.Buffered(1))


def _layer_spec(stacked, layer):
    rest = stacked.shape[1:]
    return pl.BlockSpec((None,) + rest, lambda *_: (layer,) + (0,) * len(rest),
                        pipeline_mode=pl.Buffered(1))


def _row_spec(tm, width):
    return pl.BlockSpec((tm, width), lambda i: (i, 0))


def _params(n_axes):
    return pltpu.CompilerParams(dimension_semantics=("arbitrary",) * n_axes,
                                vmem_limit_bytes=VMEM_LIMIT_BYTES)


def _pre_prompt(x, cos, sin, w, layer, kv_out, *, depth, seq, tm, tk):
    t, d = x.shape
    tiles_per_seq = seq // tm
    per_chunk = tk // tm
    n_tiles = t // tm
    cur = lambda i: jnp.minimum(i, n_tiles - 1)
    lag = lambda i: jnp.maximum(i - 1, 0)
    consts = [w[k] for k in ("w1_ffn1", "w3_ffn1", "w2_ffn1", "w_in",
                             "g_sgu", "wcat", "bfull", "wpool_t", "s_pool")]
    args = [x, cos, sin] + consts
    in_specs = [pl.BlockSpec((tm, d), lambda i: (cur(i), 0)),
                pl.BlockSpec((tm, HEAD_W), lambda i: (lag(i) % tiles_per_seq, 0)),
                pl.BlockSpec((tm, HEAD_W), lambda i: (lag(i) % tiles_per_seq, 0))]
    in_specs += [_layer_spec(c, layer) for c in consts]
    aliases = {}
    if kv_out is not None:
        aliases = {len(args): 4, len(args) + 1: 5}
        args += list(kv_out)
        in_specs += [pl.BlockSpec(memory_space=pl.ANY)] * 2
    out_shape = (
        jax.ShapeDtypeStruct((t, d), F32),
        jax.ShapeDtypeStruct((t, A_WIDTH), BF16),
        jax.ShapeDtypeStruct((t // seq, seq // tk, A_WIDTH, tk), BF16),
        jax.ShapeDtypeStruct((t, A_WIDTH), BF16),
        jax.ShapeDtypeStruct((depth, t // seq, A_WIDTH, seq), F32),
        jax.ShapeDtypeStruct((depth, t * A_HEADS, HEAD_W), F32),
        jax.ShapeDtypeStruct((t, B_WIDTH + C_WIDTH), BF16),
        jax.ShapeDtypeStruct((t // seq, HALO, C_WIDTH), F32),
    )
    lag_rows = lambda width: pl.BlockSpec((tm, width), lambda i: (lag(i), 0))
    layers, at = (depth, 0) if kv_out is None else (None, layer)
    out_specs = (
        pl.BlockSpec((tm, d), lambda i: (cur(i), 0)),
        lag_rows(A_WIDTH),
        pl.BlockSpec((None, None, A_WIDTH, tm),
                     lambda i: (lag(i) // tiles_per_seq, (lag(i) % tiles_per_seq) // per_chunk, 0,
                                lag(i) % per_chunk)),
        lag_rows(A_WIDTH),
        pl.BlockSpec((layers, None, A_WIDTH, tm),
                     lambda i: (at, lag(i) // tiles_per_seq, 0, lag(i) % tiles_per_seq)),
        pl.BlockSpec((layers, tm * A_HEADS, HEAD_W), lambda i: (at, lag(i), 0)),
        lag_rows(B_WIDTH + C_WIDTH),
        pl.BlockSpec((None, HALO, C_WIDTH), lambda i: (lag(i) // tiles_per_seq, 0, 0)),
    )
    return pl.pallas_call(
        functools.partial(_pre_prompt_kernel, tm=tm, tiles_per_seq=tiles_per_seq, n_aliased=len(aliases),
                          layer=layer),
        grid=(n_tiles + 1,),
        in_specs=in_specs,
        out_specs=out_specs,
        out_shape=out_shape,
        input_output_aliases=aliases,
        scratch_shapes=[pltpu.VMEM((tm, w["w_in"].shape[2]), F32),
                        pltpu.VMEM((tm + HALO, C_WIDTH), F32)],
        compiler_params=_params(1),
        name="pre_prompt",
    )(*args)


def _pre_sample(x, cos, sin, state, w, layer):
    m, d = x.shape
    consts = [w[k] for k in ("w1_ffn1", "w3_ffn1", "w2_ffn1", "w_in", "g_sgu", "sgu0")]
    consts += [state, w["wpool_t"], w["s_pool"]]
    args = [x, cos, sin] + consts
    out_shape = (
        jax.ShapeDtypeStruct((m, d), F32),
        jax.ShapeDtypeStruct((m, A_WIDTH), F32),
        jax.ShapeDtypeStruct((m, A_WIDTH), F32),
        jax.ShapeDtypeStruct((m, A_WIDTH), F32),
        jax.ShapeDtypeStruct((m, B_WIDTH + C_WIDTH), BF16),
        jax.ShapeDtypeStruct((m, C_WIDTH), F32),
        jax.ShapeDtypeStruct((m, B_WIDTH), F32),
    )
    return pl.pallas_call(
        _pre_sample_kernel,
        grid=(1,),
        in_specs=[_const_spec(a.shape) for a in args[:3]] + [_layer_spec(c, layer) for c in consts],
        out_specs=tuple(pl.BlockSpec(o.shape, lambda i: (0, 0)) for o in out_shape),
        out_shape=out_shape,
        compiler_params=_params(1),
        name="pre_sample",
    )(*args)


def _attention(qb, ktb, vb, q_s, k_new, v_new, cache_kt, cache_v, page_table, lam_qk, g_subln,
               *, layer, batch, seq):
    n_chunks, t = ktb.shape[1], ktb.shape[3]
    nb, n_pages = page_table.shape
    n_maps = 2 * A_HEADS
    n_steps = batch * n_chunks
    rows_per_step = nb // n_steps
    assert rows_per_step * n_steps == nb and rows_per_step % 2 == 0
    page_shape = (A_WIDTH, PAGE_SIZE)
    assert cache_kt.shape[2:] == page_shape and cache_v.shape[2:] == page_shape

    def whole(arr):
        nd = arr.ndim
        return pl.BlockSpec(arr.shape, lambda b, i, pt: (0,) * nd, pipeline_mode=pl.Buffered(1))

    grid_spec = pltpu.PrefetchScalarGridSpec(
        num_scalar_prefetch=1,
        grid=(batch, n_chunks),
        in_specs=[
            pl.BlockSpec((None,) + lam_qk.shape[1:], lambda b, i, pt: (layer, 0, 0)),
            pl.BlockSpec((None,) + g_subln.shape[1:], lambda b, i, pt: (layer, 0, 0)),
            pl.BlockSpec((t, A_WIDTH), lambda b, i, pt: (b * n_chunks + i, 0)),
            pl.BlockSpec((None, n_chunks, A_WIDTH, t), lambda b, i, pt: (b, 0, 0, 0),
                         pipeline_mode=pl.Buffered(1)),
            pl.BlockSpec((seq, A_WIDTH), lambda b, i, pt: (b, 0), pipeline_mode=pl.Buffered(1)),
            whole(q_s), whole(k_new), whole(v_new),
            pl.BlockSpec(memory_space=pl.ANY), pl.BlockSpec(memory_space=pl.ANY),
        ],
        out_specs=(pl.BlockSpec((t, A_WIDTH), lambda b, i, pt: (b * n_chunks + i, 0)),
                   pl.BlockSpec(v_new.shape, lambda b, i, pt: (0, 0, 0))),
        scratch_shapes=[pltpu.VMEM((n_maps, t, HEAD_W), BF16),
                        pltpu.VMEM((n_maps, t, HEAD_W), F32),
                        pltpu.VMEM((n_maps, t, HEAD_W), F32),
                        pltpu.VMEM((n_maps, t, HEAD_W), F32),
                        pltpu.VMEM((2, n_pages) + page_shape, F32),
                        pltpu.VMEM((2, n_pages) + page_shape, F32),
                        pltpu.SemaphoreType.DMA((2,))],
    )
    return pl.pallas_call(
        functools.partial(_attn_kernel, lam_init=_lam_init(layer), t=t, layer=layer, n_pages=n_pages,
                          rows_per_step=rows_per_step),
        grid_spec=grid_spec,
        out_shape=(jax.ShapeDtypeStruct(qb.shape, BF16), jax.ShapeDtypeStruct(v_new.shape, F32)),
        compiler_params=_params(2),
        name="attention",
    )(page_table, lam_qk, g_subln, qb, ktb, vb, q_s, k_new, v_new, cache_kt, cache_v)


def _post(x, a, bc, pe, w, layer, g_final, *, tm, final):
    t, d = x.shape
    consts = [w[k] for k in ("w_out", "w1_ffn2", "w3_ffn2", "w2_ffn2", "w_pg", "w_ple")]
    in_specs = [_row_spec(tm, d), _row_spec(tm, a.shape[1]), _row_spec(tm, bc.shape[1]),
                pl.BlockSpec((None, tm, pe.shape[2]), lambda i: (layer, i, 0))]
    in_specs += [_layer_spec(c, layer) for c in consts] + [_const_spec(g_final.shape)]
    return pl.pallas_call(
        functools.partial(_post_kernel, final=final),
        grid=(t // tm,),
        in_specs=in_specs,
        out_specs=_row_spec(tm, d),
        out_shape=jax.ShapeDtypeStruct((t, d), F32),
        compiler_params=_params(1),
        name="post",
    )(x, a, bc, pe, *consts, g_final)


def _rope_tables(pos):
    half = A_HEAD_DIM // 2
    inv = ROPE_THETA ** (-jnp.arange(half, dtype=F32) / half)
    ang = pos.astype(F32)[:, None] * inv[None, :]
    reps = HEAD_W // half
    cos = jnp.tile(jnp.cos(ang), (1, reps))
    sin = jnp.tile(jnp.sin(ang), (1, reps))
    sign = jnp.where((jnp.arange(HEAD_W) % A_HEAD_DIM) < half, -1.0, 1.0).astype(F32)
    return cos, sin * sign[None, :]


def _prepare_weights(p):
    depth = p["g_mix"].shape[0]
    row = lambda a: a.reshape(depth, 1, -1)
    bf = lambda a: a.astype(BF16)
    gained = lambda g, a: (g[:, :, None] * a).astype(BF16)
    w_sgu = p["w_sgu"]
    b_sgu = p["b_sgu"]
    per_head = B_WIDTH // B_HEADS
    return {
        "w1_ffn1": gained(p["g_ffn1"], p["w1_ffn1"]), "w3_ffn1": gained(p["g_ffn1"], p["w3_ffn1"]),
        "w2_ffn1": bf(p["w2_ffn1"]),
        "w_in": gained(p["g_mix"], p["w_in"]),
        "g_sgu": row(p["g_sgu"]),
        "wcat": jnp.transpose(w_sgu, (0, 2, 1, 3)).reshape(depth, CHUNK, B_HEADS * CHUNK),
        "bfull": jnp.repeat(jnp.transpose(b_sgu, (0, 2, 1)), per_head, axis=2),
        "sgu0": jnp.stack([jnp.repeat(w_sgu[:, :, 0, 0], per_head, axis=1),
                           jnp.repeat(b_sgu[:, :, 0], per_head, axis=1)], axis=1),
        "wpool_t": jnp.tile(p["w_pool"].reshape(depth, C_WIDTH, C_GROUP_DIM), (1, 1, len(POOL_WINDOWS))),
        "s_pool": row(p["s_pool"]),
        "w_out": bf(p["w_out"]),
        "w1_ffn2": gained(p["g_ffn2"], p["w1_ffn2"]), "w3_ffn2": gained(p["g_ffn2"], p["w3_ffn2"]),
        "w2_ffn2": bf(p["w2_ffn2"]),
        "w_pg": gained(p["g_ple"], p["w_pg"]), "w_ple": bf(p["w_ple"]),
        "lam_qk": p["lam_qk"], "g_subln": row(p["g_subln"]),
    }


def kernel(x_prompt, x_sample, cache_k, cache_v, state_pool, page_table, p_prompt, p_sample, g_ffn1, w1_ffn1, w3_ffn1, w2_ffn1, g_mix, w_in, lam_qk, g_subln, g_sgu, w_sgu, b_sgu, w_pool, s_pool, w_out, g_ffn2, w1_ffn2, w3_ffn2, w2_ffn2, g_ple, w_pg, w_ple, g_final):
    params = dict(g_ffn1=g_ffn1, w1_ffn1=w1_ffn1, w3_ffn1=w3_ffn1, w2_ffn1=w2_ffn1, g_mix=g_mix,
                  w_in=w_in, lam_qk=lam_qk, g_subln=g_subln, g_sgu=g_sgu, w_sgu=w_sgu, b_sgu=b_sgu,
                  w_pool=w_pool, s_pool=s_pool, w_out=w_out, g_ffn2=g_ffn2, w1_ffn2=w1_ffn2,
                  w3_ffn2=w3_ffn2, w2_ffn2=w2_ffn2, g_ple=g_ple, w_pg=w_pg, w_ple=w_ple)
    batch, seq, d = x_prompt.shape
    nb, dec_seq, _ = x_sample.shape
    assert dec_seq == 1
    depth = cache_k.shape[0]
    n_pool = cache_k.shape[1]
    past_len = page_table.shape[1] * PAGE_SIZE
    tm = 256
    tq = 512

    cos_p, sin_p = _rope_tables(jnp.arange(seq, dtype=jnp.int32))
    cos_s, sin_s = _rope_tables(jnp.full((nb,), past_len, dtype=jnp.int32))
    ckt = jnp.transpose(cache_k, (0, 1, 3, 4, 5, 2)).reshape(depth, n_pool, A_WIDTH, PAGE_SIZE)
    cv = cache_v.reshape(depth, n_pool, PAGE_SIZE * A_HEADS, HEAD_W)
    g_fin = g_final.reshape(1, -1)
    w = _prepare_weights(params)
    state_t = jnp.transpose(state_pool, (0, 2, 1, 3))
    pe_p = p_prompt.reshape(depth, batch * seq, -1)
    pe_s = p_sample.reshape(depth, nb, -1)

    xp = x_prompt.reshape(batch * seq, d)
    xs = x_sample.reshape(nb, d)
    kv_out = None
    ks_l, vs_l, pp_l, ps_l, sv_l = [], [], [], [], []
    for l in range(depth):
        final = l == depth - 1

        xmid, qb, ktb, vb, ktf, vf, bc, ctail = _pre_prompt(xp, cos_p, sin_p, w, l, kv_out,
                                                            depth=depth, seq=seq, tm=tm, tk=tq)
        kv_out = (ktf, vf)
        xmid_s, q, k, v, bc_s, c, sv = _pre_sample(xs, cos_s, sin_s, state_t, w, l)
        a, a_s = _attention(qb, ktb, vb, q.reshape(nb, 1, A_WIDTH), k.reshape(nb, 1, A_WIDTH),
                            v.reshape(nb, A_HEADS, HEAD_W), ckt, cv, page_table,
                            w["lam_qk"], w["g_subln"], layer=l, batch=batch, seq=seq)
        xp = _post(xmid, a, bc, pe_p, w, l, g_fin, tm=tm, final=final)
        xs = _post(xmid_s, a_s.reshape(nb, A_WIDTH), bc_s, pe_s, w, l, g_fin, tm=nb, final=final)
        pp_l.append(ctail[:, HALO - POOL_BUF:, :])
        ks_l.append(k.reshape(nb, 1, A_HEADS, 2, A_HEAD_DIM))
        vs_l.append(v.reshape(nb, 1, A_HEADS, HEAD_W))
        ps_l.append(jnp.concatenate([state_pool[l][:, 1:, :], c[:, None, :]], axis=1))
        sv_l.append(sv.reshape(nb, 1, B_WIDTH))

    ktf, vf = kv_out
    k_prompt = jnp.transpose(ktf.reshape(depth, batch, A_HEADS, 2, A_HEAD_DIM, seq), (0, 1, 5, 2, 3, 4))
    v_prompt = vf.reshape(depth, batch, seq, A_HEADS, HEAD_W)
    return (xp.reshape(batch, seq, d), xs.reshape(nb, 1, d),
            k_prompt, v_prompt, jnp.stack(ks_l), jnp.stack(vs_l),
            jnp.stack(pp_l), jnp.stack(ps_l), jnp.stack(sv_l))
```

```python
import functools
import math

import jax
import jax.numpy as jnp
from jax import lax
from jax.experimental import pallas as pl
from jax.experimental.pallas import tpu as pltpu

F32 = jnp.float32
BF16 = jnp.bfloat16

EPS = 1e-6
ROPE_THETA = 10000.0
A_HEADS = 4
A_HEAD_DIM = 64
HEAD_W = 2 * A_HEAD_DIM
A_WIDTH = A_HEADS * HEAD_W
B_WIDTH = 256
B_HEADS = 4
CHUNK = 128
C_WIDTH = 256
POOL_WINDOWS = (2, 4, 8, 16)
C_GROUP_DIM = C_WIDTH // len(POOL_WINDOWS)
HALO = 16
POOL_BUF = 15
ATTN_SCALE = A_HEAD_DIM ** -0.5
LOG2_E = math.log2(math.e)
PAGE_SIZE = 128
PAGE_SLOTS = 3

VMEM_LIMIT_BYTES = 56 * 1024 * 1024


def _lam_init(layer):
    return 0.8 - 0.6 * math.exp(-0.3 * layer)


def _rms(x, g):
    return x * lax.rsqrt(jnp.mean(x * x, axis=-1, keepdims=True) + EPS) * g


def _mm(a, w):
    return jnp.dot(a.astype(BF16), w, preferred_element_type=F32)


def _normed_mm(x, wg):
    return _mm(x, wg) * lax.rsqrt(jnp.mean(x * x, axis=-1, keepdims=True) + EPS)


def _ffn(x, w1g, w3g, w2):
    xb = x.astype(BF16)
    r = lax.rsqrt(jnp.mean(x * x, axis=-1, keepdims=True) + EPS)
    a = jnp.dot(xb, w1g, preferred_element_type=F32) * r
    b = jnp.dot(xb, w3g, preferred_element_type=F32) * r
    return x + 0.5 * _mm(a * jax.nn.sigmoid(a) * b, w2)


def _rope(z, cos, sin_signed):
    m = z.shape[0]
    lane = lax.broadcasted_iota(jnp.int32, (m, HEAD_W), 1)
    first_half = (lane % A_HEAD_DIM) < (A_HEAD_DIM // 2)
    outs = []
    for j in range(z.shape[1] // HEAD_W):
        blk = z[:, j * HEAD_W:(j + 1) * HEAD_W]
        from_right = pltpu.roll(blk, HEAD_W - A_HEAD_DIM // 2, axis=1)
        from_left = pltpu.roll(blk, A_HEAD_DIM // 2, axis=1)
        partner = jnp.where(first_half, from_right, from_left)
        outs.append(blk * cos + partner * sin_signed)
    return jnp.concatenate(outs, axis=1)


def _pool_window_lanes(shape):
    lane = lax.broadcasted_iota(jnp.int32, shape, len(shape) - 1)
    w = jnp.full(shape, POOL_WINDOWS[-1], jnp.int32)
    for gi in range(len(POOL_WINDOWS) - 2, -1, -1):
        w = jnp.where(lane < (gi + 1) * C_GROUP_DIM, POOL_WINDOWS[gi], w)
    return w


def _pool_weight(wpool_tiled):
    r = lax.broadcasted_iota(jnp.int32, wpool_tiled.shape, 0) // C_GROUP_DIM
    c = lax.broadcasted_iota(jnp.int32, wpool_tiled.shape, 1) // C_GROUP_DIM
    return jnp.where(r == c, wpool_tiled, 0.0).astype(BF16)


def _split_in(z):
    o = 0
    q = z[:, o:o + A_WIDTH]; o += A_WIDTH
    k = z[:, o:o + A_WIDTH]; o += A_WIDTH
    v = z[:, o:o + A_WIDTH]; o += A_WIDTH
    hu = z[:, o:o + B_WIDTH]; o += B_WIDTH
    hv = z[:, o:o + B_WIDTH]; o += B_WIDTH
    c = z[:, o:o + C_WIDTH]
    return q, k, v, hu, hv, c


def _pre_prompt_kernel(x_ref, cos_ref, sin_ref, w1_ref, w3_ref, w2_ref, win_ref,
                       gs_ref, wcat_ref, bfull_ref, wpool_ref, spool_ref, *rest,
                       tm, tiles_per_seq, n_aliased, layer):
    (xmid_ref, qb_ref, ktb_ref, vb_ref, ktf_ref, vf_ref, bc_ref, ctail_ref, z_ref, cext_ref) = rest[n_aliased:]
    if not n_aliased:
        for other in range(ktf_ref.shape[0]):
            if other != layer:
                ktf_ref[other] = jnp.zeros(ktf_ref.shape[1:], F32)
                vf_ref[other] = jnp.zeros(vf_ref.shape[1:], F32)
        ktf_ref = ktf_ref.at[layer]
        vf_ref = vf_ref.at[layer]
    i = pl.program_id(0)

    @pl.when(i == 0)
    def _():
        z_ref[...] = jnp.zeros(z_ref.shape, F32)
        cext_ref[...] = jnp.zeros(cext_ref.shape, F32)

    tile = jnp.maximum(i - 1, 0) % tiles_per_seq
    q, k, v, hu, hv, c = _split_in(z_ref[...])
    cos = cos_ref[...]
    sin = sin_ref[...]
    q = _rope(q, cos, sin)
    k = _rope(k, cos, sin)
    qb_ref[...] = (q * (ATTN_SCALE * LOG2_E)).astype(BF16)
    kt = k.T
    ktf_ref[...] = kt
    ktb_ref[...] = kt.astype(BF16)
    vb_ref[...] = v.astype(BF16)
    for h in range(A_HEADS):
        vf_ref[pl.ds(h, tm, stride=A_HEADS), :] = v[:, h * HEAD_W:(h + 1) * HEAD_W]
    u = jax.nn.gelu(hu)
    vn = _rms(jax.nn.gelu(hv), gs_ref[...])

    cext_ref[0:HALO, :] = jnp.where(tile == 0, 0.0, cext_ref[tm:tm + HALO, :])
    cext_ref[HALO:HALO + tm, :] = c
    ctail_ref[...] = c[tm - HALO:tm, :]
    acc = c
    sums = {}
    for j in range(1, POOL_WINDOWS[-1]):
        acc = acc + cext_ref[HALO - j:HALO - j + tm, :]
        if (j + 1) in POOL_WINDOWS:
            sums[j + 1] = acc
    wl = _pool_window_lanes((tm, C_WIDTH))
    s = sums[POOL_WINDOWS[-1]]
    for w in POOL_WINDOWS[-2::-1]:
        s = jnp.where(wl == w, sums[w], s)
    pos = tile * tm + lax.broadcasted_iota(jnp.int32, (tm, C_WIDTH), 0)
    cnt = jnp.minimum(pos + 1, wl).astype(F32)
    d = s / cnt - c

    xm = _ffn(x_ref[...], w1_ref[...], w3_ref[...], w2_ref[...])
    xmid_ref[...] = xm
    z_ref[...] = _normed_mm(xm, win_ref[...])

    wr = lax.broadcasted_iota(jnp.int32, (CHUNK, B_HEADS * CHUNK), 0)
    wc = lax.broadcasted_iota(jnp.int32, (CHUNK, B_HEADS * CHUNK), 1)
    wm = jnp.where((wc % CHUNK) <= wr, wcat_ref[...], 0.0).astype(BF16)
    hr = lax.broadcasted_iota(jnp.int32, (B_HEADS * CHUNK, B_WIDTH), 0) // CHUNK
    hc = lax.broadcasted_iota(jnp.int32, (B_HEADS * CHUNK, B_WIDTH), 1) // (B_WIDTH // B_HEADS)
    head_mask = hr == hc
    bfull = bfull_ref[...]
    for ci in range(tm // CHUNK):
        rows = slice(ci * CHUNK, (ci + 1) * CHUNK)
        vc = vn[rows].astype(BF16)
        vbd = jnp.where(head_mask, jnp.concatenate([vc] * B_HEADS, axis=0), jnp.zeros((), BF16))
        zc = jnp.dot(wm, vbd, preferred_element_type=F32) + bfull
        bc_ref[rows, 0:B_WIDTH] = (u[rows] * zc).astype(BF16)
    co = _mm(d, _pool_weight(wpool_ref[...])) * spool_ref[...]
    bc_ref[:, B_WIDTH:B_WIDTH + C_WIDTH] = co.astype(BF16)


def _pre_sample_kernel(x_ref, cos_ref, sin_ref, w1_ref, w3_ref, w2_ref, win_ref,
                       gs_ref, sgu0_ref, state_ref, wpool_ref, spool_ref,
                       xmid_ref, q_ref, k_ref, v_ref, bc_ref, c_ref, sv_ref):
    xm = _ffn(x_ref[...], w1_ref[...], w3_ref[...], w2_ref[...])
    xmid_ref[...] = xm
    z = _normed_mm(xm, win_ref[...])
    q, k, v, hu, hv, c = _split_in(z)
    cos = cos_ref[...]
    sin = sin_ref[...]
    q_ref[...] = _rope(q, cos, sin) * ATTN_SCALE
    k_ref[...] = _rope(k, cos, sin)
    v_ref[...] = v
    c_ref[...] = c

    u = jax.nn.gelu(hu)
    vn = _rms(jax.nn.gelu(hv), gs_ref[...])
    sv_ref[...] = vn
    sgu0 = sgu0_ref[...]
    bc_ref[:, 0:B_WIDTH] = (u * (vn * sgu0[0:1, :] + sgu0[1:2, :])).astype(BF16)

    state = state_ref[...]
    wl3 = _pool_window_lanes(state.shape)
    r3 = lax.broadcasted_iota(jnp.int32, state.shape, 0)
    s = jnp.sum(jnp.where(r3 >= (POOL_BUF + 1) - wl3, state, 0.0), axis=0) + c
    d = s / _pool_window_lanes(c.shape).astype(F32) - c
    co = _mm(d, _pool_weight(wpool_ref[...])) * spool_ref[...]
    bc_ref[:, B_WIDTH:B_WIDTH + C_WIDTH] = co.astype(BF16)


def _lambda(lam_ref, lam_init):
    lq = lam_ref[...]
    s01 = jnp.sum(lq[0:1, :] * lq[1:2, :], axis=1, keepdims=True)
    s23 = jnp.sum(lq[2:3, :] * lq[3:4, :], axis=1, keepdims=True)
    return jnp.exp(s01) - jnp.exp(s23) + lam_init


def _attn_kernel(pt_ref, lam_ref, g_ref, q_ref, kt_ref, v_ref, qs_ref, kn_ref, vn_ref, ck_hbm, cv_hbm,
                 o_ref, os_ref, qz_ref, m_ref, l_ref, acc_ref, kbuf, vbuf, sem,
                 *, lam_init, t, layer, n_pages, rows_per_step):
    qi = pl.program_id(1)
    step = pl.program_id(0) * pl.num_programs(1) + qi
    n_maps = 2 * A_HEADS

    n_slots = kbuf.shape[0]
    n_rows = os_ref.shape[0]

    def page_copies(r):
        slot = r % n_slots
        copies = []
        for j in range(n_pages):
            page = pt_ref[r, j]
            copies.append(pltpu.make_async_copy(ck_hbm.at[layer, page], kbuf.at[slot, j], sem.at[slot]))
            copies.append(pltpu.make_async_copy(cv_hbm.at[layer, page], vbuf.at[slot, j], sem.at[slot]))
        return copies

    def start_row(r):
        for cp in page_copies(r):
            cp.start()

    def sample_row(r):
        for cp in page_copies(r):
            cp.wait()
        k_slot = kbuf.at[r % n_slots]
        v_slot = vbuf.at[r % n_slots]
        k_pages = [k_slot[j] for j in range(n_pages)]
        v_rows = lambda h: [v_slot[j, pl.ds(h, PAGE_SIZE, stride=A_HEADS), :] for j in range(n_pages)]
        os_ref[r] = _sample_attend(qs_ref[r], kn_ref[r], vn_ref[r], k_pages, v_rows,
                                   _lambda(lam_ref, lam_init), g_ref[...], lam_init)

        @pl.when(r + n_slots < n_rows)
        def _():
            start_row(r + n_slots)

    first_row = step * rows_per_step

    @pl.when(step == 0)
    def _():
        for r in range(n_slots):
            start_row(r)

    def sample_rows(lo, hi):
        for i in range(lo, hi):
            sample_row(first_row + i)
    lane = lax.broadcasted_iota(jnp.int32, (t, HEAD_W), 1)
    zero = jnp.zeros((), BF16)
    for h in range(A_HEADS):
        qh = q_ref[:, h * HEAD_W:(h + 1) * HEAD_W]
        qz_ref[2 * h] = jnp.where(lane < A_HEAD_DIM, qh, zero)
        qz_ref[2 * h + 1] = jnp.where(lane >= A_HEAD_DIM, qh, zero)

    def fold(j, rows, keys, diagonal):
        r0, nr = rows
        k0, nk = keys
        start = pl.multiple_of(j * t, t) + k0

        def scores(i):
            h = i // 2
            s = jnp.dot(qz_ref[i, r0:r0 + nr, :], kt_ref[j, h * HEAD_W:(h + 1) * HEAD_W, k0:k0 + nk],
                        preferred_element_type=F32)
            if diagonal:
                row = lax.broadcasted_iota(jnp.int32, s.shape, 0) + r0
                col = lax.broadcasted_iota(jnp.int32, s.shape, 1) + k0
                s = jnp.where(col <= row, s, -jnp.inf)
            return s

        s_next = scores(0)
        for i in range(n_maps):
            s = s_next
            if i + 1 < n_maps:
                s_next = scores(i + 1)
            h = i // 2
            vc = v_ref[pl.ds(start, nk), h * HEAD_W:(h + 1) * HEAD_W]
            cols = [s[:, cc * HEAD_W:(cc + 1) * HEAD_W] for cc in range(nk // HEAD_W)]
            m_new = jnp.broadcast_to(jnp.max(functools.reduce(jnp.maximum, cols), axis=1, keepdims=True),
                                     (nr, HEAD_W))
            if not diagonal:
                m_old = m_ref[i, r0:r0 + nr, :]
                m_new = jnp.maximum(m_old, m_new)
                alpha = jnp.exp2(m_old - m_new)
            ps = [jnp.exp2((cl - m_new).astype(BF16)) for cl in cols]
            l_new = functools.reduce(jnp.add, ps).astype(F32)
            pv = jnp.dot(jnp.concatenate(ps, axis=1), vc, preferred_element_type=F32)
            if not diagonal:
                l_new = alpha * l_ref[i, r0:r0 + nr, :] + l_new
                pv = alpha * acc_ref[i, r0:r0 + nr, :] + pv
            l_ref[i, r0:r0 + nr, :] = l_new
            acc_ref[i, r0:r0 + nr, :] = pv
            m_ref[i, r0:r0 + nr, :] = m_new

    def body(j, carry):
        fold(j, (0, t), (0, t), False)
        return carry

    def diagonal_chunk():
        fold(qi, (0, t // 2), (0, t // 2), True)
        fold(qi, (t // 2, t // 2), (0, t), True)

    half = rows_per_step // 2
    sample_rows(0, half)
    diagonal_chunk()
    sample_rows(half, rows_per_step)
    lax.fori_loop(0, qi, body, 0)
    lam = _lambda(lam_ref, lam_init)
    g = g_ref[...]
    for h in range(A_HEADS):
        l0 = jnp.sum(l_ref[2 * h], axis=1, keepdims=True)
        l1 = jnp.sum(l_ref[2 * h + 1], axis=1, keepdims=True)
        o = acc_ref[2 * h] / l0 - lam * (acc_ref[2 * h + 1] / l1)
        o_ref[:, h * HEAD_W:(h + 1) * HEAD_W] = (_rms(o, g) * (1.0 - lam_init)).astype(BF16)


def _sample_attend(q, k_new, v_new, k_pages, v_rows, lam, g, lam_init):
    n_maps = 2 * A_HEADS
    row = lax.broadcasted_iota(jnp.int32, (n_maps, A_WIDTH), 0)
    lane = lax.broadcasted_iota(jnp.int32, (n_maps, A_WIDTH), 1)
    owner = (row % A_HEADS) * 2 + row // A_HEADS
    qbd = jnp.where(lane // A_HEAD_DIM == owner, jnp.broadcast_to(q, (n_maps, A_WIDTH)), 0.0)
    qbd_b = qbd.astype(BF16)
    kt = jnp.concatenate([kp.astype(BF16) for kp in k_pages], axis=1)
    s = jnp.dot(qbd_b, kt, preferred_element_type=F32)
    s_new = jnp.sum(qbd * k_new, axis=1, keepdims=True)
    m = jnp.maximum(jnp.max(s, axis=1, keepdims=True), s_new)
    e = jnp.exp(s - m)
    e_new = jnp.exp(s_new - m)
    l = jnp.sum(e, axis=1, keepdims=True) + e_new
    p = e / l
    p_new = e_new / l
    a = (p[0:A_HEADS] - lam * p[A_HEADS:n_maps]).astype(BF16)
    a_new = p_new[0:A_HEADS] - lam * p_new[A_HEADS:n_maps]
    outs = []
    for h in range(A_HEADS):
        vh = jnp.concatenate([vp.astype(BF16) for vp in v_rows(h)], axis=0)
        outs.append(jnp.dot(a[h:h + 1, :], vh, preferred_element_type=F32))
    o = jnp.concatenate(outs, axis=0) + a_new * v_new
    return _rms(o, g) * (1.0 - lam_init)


def _post_kernel(x_ref, a_ref, bc_ref, pe_ref, wo_ref, w1_ref, w3_ref, w2_ref,
                 wpg_ref, wple_ref, gf_ref, o_ref, *, final):
    x = x_ref[...]
    x = x + _mm(a_ref[...], wo_ref[0:A_WIDTH, :]) + _mm(bc_ref[...], wo_ref[A_WIDTH:, :])
    x = _ffn(x, w1_ref[...], w3_ref[...], w2_ref[...])
    gate = jax.nn.sigmoid(_normed_mm(x, wpg_ref[...]))
    x = x + gate * _mm(pe_ref[...], wple_ref[...])
    if final:
        x = _rms(x, gf_ref[...])
    o_ref[...] = x


def _const_spec(shape):
    nd = len(shape)
    return pl.BlockSpec(shape, lambda *_: (0,) * nd, pipeline_mode=pl.Buffered(1))


def _layer_spec(stacked, layer):
    rest = stacked.shape[1:]
    return pl.BlockSpec((None,) + rest, lambda *_: (layer,) + (0,) * len(rest),
                        pipeline_mode=pl.Buffered(1))


def _row_spec(tm, width):
    return pl.BlockSpec((tm, width), lambda i: (i, 0))


def _params(n_axes):
    return pltpu.CompilerParams(dimension_semantics=("arbitrary",) * n_axes,
                                vmem_limit_bytes=VMEM_LIMIT_BYTES)


def _pre_prompt(x, cos, sin, w, layer, kv_out, *, depth, seq, tm, tk):
    t, d = x.shape
    tiles_per_seq = seq // tm
    per_chunk = tk // tm
    n_tiles = t // tm
    cur = lambda i: jnp.minimum(i, n_tiles - 1)
    lag = lambda i: jnp.maximum(i - 1, 0)
    consts = [w[k] for k in ("w1_ffn1", "w3_ffn1", "w2_ffn1", "w_in",
                             "g_sgu", "wcat", "bfull", "wpool_t", "s_pool")]
    args = [x, cos, sin] + consts
    in_specs = [pl.BlockSpec((tm, d), lambda i: (cur(i), 0)),
                pl.BlockSpec((tm, HEAD_W), lambda i: (lag(i) % tiles_per_seq, 0)),
                pl.BlockSpec((tm, HEAD_W), lambda i: (lag(i) % tiles_per_seq, 0))]
    in_specs += [_layer_spec(c, layer) for c in consts]
    aliases = {}
    if kv_out is not None:
        aliases = {len(args): 4, len(args) + 1: 5}
        args += list(kv_out)
        in_specs += [pl.BlockSpec(memory_space=pl.ANY)] * 2
    out_shape = (
        jax.ShapeDtypeStruct((t, d), F32),
        jax.ShapeDtypeStruct((t, A_WIDTH), BF16),
        jax.ShapeDtypeStruct((t // seq, seq // tk, A_WIDTH, tk), BF16),
        jax.ShapeDtypeStruct((t, A_WIDTH), BF16),
        jax.ShapeDtypeStruct((depth, t // seq, A_WIDTH, seq), F32),
        jax.ShapeDtypeStruct((depth, t * A_HEADS, HEAD_W), F32),
        jax.ShapeDtypeStruct((t, B_WIDTH + C_WIDTH), BF16),
        jax.ShapeDtypeStruct((t // seq, HALO, C_WIDTH), F32),
    )
    lag_rows = lambda width: pl.BlockSpec((tm, width), lambda i: (lag(i), 0))
    layers, at = (depth, 0) if kv_out is None else (None, layer)
    out_specs = (
        pl.BlockSpec((tm, d), lambda i: (cur(i), 0)),
        lag_rows(A_WIDTH),
        pl.BlockSpec((None, None, A_WIDTH, tm),
                     lambda i: (lag(i) // tiles_per_seq, (lag(i) % tiles_per_seq) // per_chunk, 0,
                                lag(i) % per_chunk)),
        lag_rows(A_WIDTH),
        pl.BlockSpec((layers, None, A_WIDTH, tm),
                     lambda i: (at, lag(i) // tiles_per_seq, 0, lag(i) % tiles_per_seq)),
        pl.BlockSpec((layers, tm * A_HEADS, HEAD_W), lambda i: (at, lag(i), 0)),
        lag_rows(B_WIDTH + C_WIDTH),
        pl.BlockSpec((None, HALO, C_WIDTH), lambda i: (lag(i) // tiles_per_seq, 0, 0)),
    )
    return pl.pallas_call(
        functools.partial(_pre_prompt_kernel, tm=tm, tiles_per_seq=tiles_per_seq, n_aliased=len(aliases),
                          layer=layer),
        grid=(n_tiles + 1,),
        in_specs=in_specs,
        out_specs=out_specs,
        out_shape=out_shape,
        input_output_aliases=aliases,
        scratch_shapes=[pltpu.VMEM((tm, w["w_in"].shape[2]), F32),
                        pltpu.VMEM((tm + HALO, C_WIDTH), F32)],
        compiler_params=_params(1),
        name="pre_prompt",
    )(*args)


def _pre_sample(x, cos, sin, state, w, layer):
    m, d = x.shape
    consts = [w[k] for k in ("w1_ffn1", "w3_ffn1", "w2_ffn1", "w_in", "g_sgu", "sgu0")]
    consts += [state, w["wpool_t"], w["s_pool"]]
    args = [x, cos, sin] + consts
    out_shape = (
        jax.ShapeDtypeStruct((m, d), F32),
        jax.ShapeDtypeStruct((m, A_WIDTH), F32),
        jax.ShapeDtypeStruct((m, A_WIDTH), F32),
        jax.ShapeDtypeStruct((m, A_WIDTH), F32),
        jax.ShapeDtypeStruct((m, B_WIDTH + C_WIDTH), BF16),
        jax.ShapeDtypeStruct((m, C_WIDTH), F32),
        jax.ShapeDtypeStruct((m, B_WIDTH), F32),
    )
    return pl.pallas_call(
        _pre_sample_kernel,
        grid=(1,),
        in_specs=[_const_spec(a.shape) for a in args[:3]] + [_layer_spec(c, layer) for c in consts],
        out_specs=tuple(pl.BlockSpec(o.shape, lambda i: (0, 0)) for o in out_shape),
        out_shape=out_shape,
        compiler_params=_params(1),
        name="pre_sample",
    )(*args)


def _attention(qb, ktb, vb, q_s, k_new, v_new, cache_kt, cache_v, page_table, lam_qk, g_subln,
               *, layer, batch, seq):
    n_chunks, t = ktb.shape[1], ktb.shape[3]
    nb, n_pages = page_table.shape
    n_maps = 2 * A_HEADS
    n_steps = batch * n_chunks
    rows_per_step = nb // n_steps
    assert rows_per_step * n_steps == nb and rows_per_step % 2 == 0
    page_shape = (A_WIDTH, PAGE_SIZE)
    assert cache_kt.shape[2:] == page_shape and cache_v.shape[2:] == page_shape

    def whole(arr):
        nd = arr.ndim
        return pl.BlockSpec(arr.shape, lambda b, i, pt: (0,) * nd, pipeline_mode=pl.Buffered(1))

    grid_spec = pltpu.PrefetchScalarGridSpec(
        num_scalar_prefetch=1,
        grid=(batch, n_chunks),
        in_specs=[
            pl.BlockSpec((None,) + lam_qk.shape[1:], lambda b, i, pt: (layer, 0, 0)),
            pl.BlockSpec((None,) + g_subln.shape[1:], lambda b, i, pt: (layer, 0, 0)),
            pl.BlockSpec((t, A_WIDTH), lambda b, i, pt: (b * n_chunks + i, 0)),
            pl.BlockSpec((None, n_chunks, A_WIDTH, t), lambda b, i, pt: (b, 0, 0, 0),
                         pipeline_mode=pl.Buffered(1)),
            pl.BlockSpec((seq, A_WIDTH), lambda b, i, pt: (b, 0), pipeline_mode=pl.Buffered(1)),
            whole(q_s), whole(k_new), whole(v_new),
            pl.BlockSpec(memory_space=pl.ANY), pl.BlockSpec(memory_space=pl.ANY),
        ],
        out_specs=(pl.BlockSpec((t, A_WIDTH), lambda b, i, pt: (b * n_chunks + i, 0)),
                   pl.BlockSpec(v_new.shape, lambda b, i, pt: (0, 0, 0))),
        scratch_shapes=[pltpu.VMEM((n_maps, t, HEAD_W), BF16),
                        pltpu.VMEM((n_maps, t, HEAD_W), F32),
                        pltpu.VMEM((n_maps, t, HEAD_W), F32),
                        pltpu.VMEM((n_maps, t, HEAD_W), F32),
                        pltpu.VMEM((PAGE_SLOTS, n_pages) + page_shape, F32),
                        pltpu.VMEM((PAGE_SLOTS, n_pages) + page_shape, F32),
                        pltpu.SemaphoreType.DMA((PAGE_SLOTS,))],
    )
    return pl.pallas_call(
        functools.partial(_attn_kernel, lam_init=_lam_init(layer), t=t, layer=layer, n_pages=n_pages,
                          rows_per_step=rows_per_step),
        grid_spec=grid_spec,
        out_shape=(jax.ShapeDtypeStruct(qb.shape, BF16), jax.ShapeDtypeStruct(v_new.shape, F32)),
        compiler_params=_params(2),
        name="attention",
    )(page_table, lam_qk, g_subln, qb, ktb, vb, q_s, k_new, v_new, cache_kt, cache_v)


def _post(x, a, bc, pe, w, layer, g_final, *, tm, final):
    t, d = x.shape
    consts = [w[k] for k in ("w_out", "w1_ffn2", "w3_ffn2", "w2_ffn2", "w_pg", "w_ple")]
    in_specs = [_row_spec(tm, d), _row_spec(tm, a.shape[1]), _row_spec(tm, bc.shape[1]),
                pl.BlockSpec((None, tm, pe.shape[2]), lambda i: (layer, i, 0))]
    in_specs += [_layer_spec(c, layer) for c in consts] + [_const_spec(g_final.shape)]
    return pl.pallas_call(
        functools.partial(_post_kernel, final=final),
        grid=(t // tm,),
        in_specs=in_specs,
        out_specs=_row_spec(tm, d),
        out_shape=jax.ShapeDtypeStruct((t, d), F32),
        compiler_params=_params(1),
        name="post",
    )(x, a, bc, pe, *consts, g_final)


def _rope_tables(pos):
    half = A_HEAD_DIM // 2
    inv = ROPE_THETA ** (-jnp.arange(half, dtype=F32) / half)
    ang = pos.astype(F32)[:, None] * inv[None, :]
    reps = HEAD_W // half
    cos = jnp.tile(jnp.cos(ang), (1, reps))
    sin = jnp.tile(jnp.sin(ang), (1, reps))
    sign = jnp.where((jnp.arange(HEAD_W) % A_HEAD_DIM) < half, -1.0, 1.0).astype(F32)
    return cos, sin * sign[None, :]


def _prepare_weights(p):
    depth = p["g_mix"].shape[0]
    row = lambda a: a.reshape(depth, 1, -1)
    bf = lambda a: a.astype(BF16)
    gained = lambda g, a: (g[:, :, None] * a).astype(BF16)
    w_sgu = p["w_sgu"]
    b_sgu = p["b_sgu"]
    per_head = B_WIDTH // B_HEADS
    return {
        "w1_ffn1": gained(p["g_ffn1"], p["w1_ffn1"]), "w3_ffn1": gained(p["g_ffn1"], p["w3_ffn1"]),
        "w2_ffn1": bf(p["w2_ffn1"]),
        "w_in": gained(p["g_mix"], p["w_in"]),
        "g_sgu": row(p["g_sgu"]),
        "wcat": jnp.transpose(w_sgu, (0, 2, 1, 3)).reshape(depth, CHUNK, B_HEADS * CHUNK),
        "bfull": jnp.repeat(jnp.transpose(b_sgu, (0, 2, 1)), per_head, axis=2),
        "sgu0": jnp.stack([jnp.repeat(w_sgu[:, :, 0, 0], per_head, axis=1),
                           jnp.repeat(b_sgu[:, :, 0], per_head, axis=1)], axis=1),
        "wpool_t": jnp.tile(p["w_pool"].reshape(depth, C_WIDTH, C_GROUP_DIM), (1, 1, len(POOL_WINDOWS))),
        "s_pool": row(p["s_pool"]),
        "w_out": bf(p["w_out"]),
        "w1_ffn2": gained(p["g_ffn2"], p["w1_ffn2"]), "w3_ffn2": gained(p["g_ffn2"], p["w3_ffn2"]),
        "w2_ffn2": bf(p["w2_ffn2"]),
        "w_pg": gained(p["g_ple"], p["w_pg"]), "w_ple": bf(p["w_ple"]),
        "lam_qk": p["lam_qk"], "g_subln": row(p["g_subln"]),
    }


def kernel(x_prompt, x_sample, cache_k, cache_v, state_pool, page_table, p_prompt, p_sample, g_ffn1, w1_ffn1, w3_ffn1, w2_ffn1, g_mix, w_in, lam_qk, g_subln, g_sgu, w_sgu, b_sgu, w_pool, s_pool, w_out, g_ffn2, w1_ffn2, w3_ffn2, w2_ffn2, g_ple, w_pg, w_ple, g_final):
    params = dict(g_ffn1=g_ffn1, w1_ffn1=w1_ffn1, w3_ffn1=w3_ffn1, w2_ffn1=w2_ffn1, g_mix=g_mix,
                  w_in=w_in, lam_qk=lam_qk, g_subln=g_subln, g_sgu=g_sgu, w_sgu=w_sgu, b_sgu=b_sgu,
                  w_pool=w_pool, s_pool=s_pool, w_out=w_out, g_ffn2=g_ffn2, w1_ffn2=w1_ffn2,
                  w3_ffn2=w3_ffn2, w2_ffn2=w2_ffn2, g_ple=g_ple, w_pg=w_pg, w_ple=w_ple)
    batch, seq, d = x_prompt.shape
    nb, dec_seq, _ = x_sample.shape
    assert dec_seq == 1
    depth = cache_k.shape[0]
    n_pool = cache_k.shape[1]
    past_len = page_table.shape[1] * PAGE_SIZE
    tm = 256
    tq = 512

    cos_p, sin_p = _rope_tables(jnp.arange(seq, dtype=jnp.int32))
    cos_s, sin_s = _rope_tables(jnp.full((nb,), past_len, dtype=jnp.int32))
    ckt = jnp.transpose(cache_k, (0, 1, 3, 4, 5, 2)).reshape(depth, n_pool, A_WIDTH, PAGE_SIZE)
    cv = cache_v.reshape(depth, n_pool, PAGE_SIZE * A_HEADS, HEAD_W)
    g_fin = g_final.reshape(1, -1)
    w = _prepare_weights(params)
    state_t = jnp.transpose(state_pool, (0, 2, 1, 3))
    pe_p = p_prompt.reshape(depth, batch * seq, -1)
    pe_s = p_sample.reshape(depth, nb, -1)

    xp = x_prompt.reshape(batch * seq, d)
    xs = x_sample.reshape(nb, d)
    kv_out = None
    ks_l, vs_l, pp_l, ps_l, sv_l = [], [], [], [], []
    for l in range(depth):
        final = l == depth - 1

        xmid, qb, ktb, vb, ktf, vf, bc, ctail = _pre_prompt(xp, cos_p, sin_p, w, l, kv_out,
                                                            depth=depth, seq=seq, tm=tm, tk=tq)
        kv_out = (ktf, vf)
        xmid_s, q, k, v, bc_s, c, sv = _pre_sample(xs, cos_s, sin_s, state_t, w, l)
        a, a_s = _attention(qb, ktb, vb, q.reshape(nb, 1, A_WIDTH), k.reshape(nb, 1, A_WIDTH),
                            v.reshape(nb, A_HEADS, HEAD_W), ckt, cv, page_table,
                            w["lam_qk"], w["g_subln"], layer=l, batch=batch, seq=seq)
        xp = _post(xmid, a, bc, pe_p, w, l, g_fin, tm=tm, final=final)
        xs = _post(xmid_s, a_s.reshape(nb, A_WIDTH), bc_s, pe_s, w, l, g_fin, tm=nb, final=final)
        pp_l.append(ctail[:, HALO - POOL_BUF:, :])
        ks_l.append(k.reshape(nb, 1, A_HEADS, 2, A_HEAD_DIM))
        vs_l.append(v.reshape(nb, 1, A_HEADS, HEAD_W))
        ps_l.append(jnp.concatenate([state_pool[l][:, 1:, :], c[:, None, :]], axis=1))
        sv_l.append(sv.reshape(nb, 1, B_WIDTH))

    ktf, vf = kv_out
    k_prompt = jnp.transpose(ktf.reshape(depth, batch, A_HEADS, 2, A_HEAD_DIM, seq), (0, 1, 5, 2, 3, 4))
    v_prompt = vf.reshape(depth, batch, seq, A_HEADS, HEAD_W)
    return (xp.reshape(batch, seq, d), xs.reshape(nb, 1, d),
            k_prompt, v_prompt, jnp.stack(ks_l), jnp.stack(vs_l),
            jnp.stack(pp_l), jnp.stack(ps_l), jnp.stack(sv_l))
```
